```python
import jax, jax.numpy as jnp
from jax import lax
import numpy as np

D_MODEL = 1024
BATCH = 8
SEQ = 4096
DEPTH = 4

GRID_W = 64
CTX_LEN = 256
EPS = 1e-6
A_WIDTH = D_MODEL // 2
A_CONV_W = 31
B_GROUPS = 4
B_WIDTH = D_MODEL // 2
B_GROUP_DIM = B_WIDTH // B_GROUPS
C_WIDTH = D_MODEL // 2
C_CONV_W = 3
HEAD_DIM = 64
N_Q_HEADS = (D_MODEL // 2) // HEAD_DIM
N_KV_HEADS = max(N_Q_HEADS // 4, 1)
Q_GROUP = N_Q_HEADS // N_KV_HEADS
ATT_WIDTH = N_Q_HEADS * HEAD_DIM
KV_WIDTH = N_KV_HEADS * HEAD_DIM
WINDOW = 128
BLOCK = 128
ROPE_BASE = 10000.0
NEG_INF = -1e30
EVEN_IN = 3 * A_WIDTH + 2 * B_WIDTH
EVEN_MIX = A_WIDTH + B_WIDTH
ODD_IN = 4 * C_WIDTH + 2 * ATT_WIDTH + 2 * KV_WIDTH
ODD_MIX = C_WIDTH + ATT_WIDTH

kernel_name = 'hybrid_conformer_fnet_shortconv_swa_dit'


def _rmsnorm(x, g):
    xf = x.astype(jnp.float32)
    y = xf * lax.rsqrt(jnp.mean(xf * xf, axis=-1, keepdims=True) + EPS)
    return (y * g.astype(jnp.float32)).astype(x.dtype)


def _layernorm(x, g, b):
    xf = x.astype(jnp.float32)
    mu = jnp.mean(xf, axis=-1, keepdims=True)
    var = jnp.mean(jnp.square(xf - mu), axis=-1, keepdims=True)
    y = (xf - mu) * lax.rsqrt(var + EPS)
    return (y * g.astype(jnp.float32) + b.astype(jnp.float32)).astype(x.dtype)


def _split(t, widths):
    cuts, acc = [], 0
    for w in widths[:-1]:
        acc += w
        cuts.append(acc)
    return jnp.split(t, cuts, axis=-1)


def _modulation(cvec, w_mod, b_mod):
    h = jax.nn.silu(cvec) @ w_mod + b_mod
    shift, scale, gate = jnp.split(h[:, None, :], 3, axis=-1)
    return shift, scale, gate


def _pre(x, g, shift, scale):
    return _rmsnorm(x, g) * (1 + scale) + shift


def _dwconv(u, w):
    pad = (w.shape[0] - 1) // 2
    return lax.conv_general_dilated(
        u, w[:, None, :].astype(u.dtype), window_strides=(1,), padding=[(pad, pad)],
        dimension_numbers=('NWC', 'WIO', 'NWC'), feature_group_count=u.shape[-1])


def _fourier_mix(v, fw):
    b, l, _ = v.shape
    vg = v.reshape(b, l, B_GROUPS, B_GROUP_DIM).astype(jnp.float32)
    f = jnp.fft.fftn(vg, axes=(1, 3), norm='ortho').real.astype(v.dtype)
    return jnp.einsum('blgd,gde->blge', f, fw).reshape(b, l, B_WIDTH)


def _even_mix(h, w_in, a_conv_w, a_conv_b, a_ln_g, a_ln_b, a_pw_w, a_pw_b, b_fw, w_out):
    a_val, a_glu, a_gate, b_val, b_gate = _split(h @ w_in, [A_WIDTH, A_WIDTH, A_WIDTH, B_WIDTH, B_WIDTH])
    u = a_val * jax.nn.sigmoid(a_glu)
    u = jax.nn.silu(_layernorm(_dwconv(u, a_conv_w) + a_conv_b, a_ln_g, a_ln_b))
    y_a = (u @ a_pw_w + a_pw_b) * jax.nn.silu(a_gate)
    y_b = _fourier_mix(b_val, b_fw) * jax.nn.silu(b_gate)
    return jnp.concatenate([y_a, y_b], axis=-1) @ w_out


def _rope_tables(row, col):
    nf = HEAD_DIM // 4
    inv = jnp.power(ROPE_BASE, -jnp.arange(nf, dtype=jnp.float32) / nf)
    ang = jnp.stack([row.astype(jnp.float32)[:, None] * inv,
                     col.astype(jnp.float32)[:, None] * inv], axis=1)
    return jnp.cos(ang), jnp.sin(ang)


def _apply_rope(t, cos, sin):
    b, l, h, d = t.shape
    tr = t.astype(jnp.float32).reshape(b, l, h, 2, 2, d // 4)
    t1, t2 = tr[..., 0, :], tr[..., 1, :]
    cs, sn = cos[None, :, None], sin[None, :, None]
    out = jnp.stack([t1 * cs - t2 * sn, t2 * cs + t1 * sn], axis=-2)
    return out.reshape(b, l, h, d).astype(t.dtype)


def _sink_softmax(logits, sink):
    m = sink
    for s in logits:
        m = jnp.maximum(m, jnp.max(s, axis=-1, keepdims=True))
    ps = [jnp.exp(s - m) for s in logits]
    denom = jnp.exp(sink - m)
    for p in ps:
        denom = denom + jnp.sum(p, axis=-1, keepdims=True)
    return [p / denom for p in ps]


def _window_attention(q, k, v, kc, vc, sink):
    b, l, hq, hd = q.shape
    nb = l // BLOCK
    scale = HEAD_DIM ** -0.5
    qb = q.reshape(b, nb, BLOCK, N_KV_HEADS, Q_GROUP, hd)

    def band(t):
        tp = jnp.pad(t, ((0, 0), (BLOCK, BLOCK), (0, 0), (0, 0))).reshape(b, nb + 2, BLOCK, N_KV_HEADS, hd)
        return jnp.concatenate([tp[:, :-2], tp[:, 1:-1], tp[:, 2:]], axis=2)

    kw, vw = band(k), band(v)
    qpos = jnp.arange(l).reshape(nb, BLOCK)
    kpos = jnp.arange(-BLOCK, l + BLOCK).reshape(nb + 2, BLOCK)
    kwpos = jnp.concatenate([kpos[:-2], kpos[1:-1], kpos[2:]], axis=1)
    valid = ((jnp.abs(qpos[:, :, None] - kwpos[:, None, :]) <= WINDOW)
             & (kwpos >= 0)[:, None, :] & (kwpos < l)[:, None, :])
    s_win = jnp.einsum('bnqhgd,bnkhd->bnhgqk', qb, kw).astype(jnp.float32) * scale
    s_win = jnp.where(valid[None, :, None, None], s_win, NEG_INF)
    s_ctx = jnp.einsum('bnqhgd,bchd->bnhgqc', qb, kc).astype(jnp.float32) * scale
    sink_b = sink.astype(jnp.float32).reshape(N_KV_HEADS, Q_GROUP)[None, None, :, :, None, None]
    p_win, p_ctx = _sink_softmax([s_win, s_ctx], sink_b)
    o = (jnp.einsum('bnhgqk,bnkhd->bnqhgd', p_win.astype(v.dtype), vw)
         + jnp.einsum('bnhgqc,bchd->bnqhgd', p_ctx.astype(v.dtype), vc))
    return o.reshape(b, l, hq * hd)


def _context_attention(q, k, v, sink):
    b, lc, hq, hd = q.shape
    qg = q.reshape(b, lc, N_KV_HEADS, Q_GROUP, hd)
    s = jnp.einsum('bqhgd,bkhd->bhgqk', qg, k).astype(jnp.float32) * (HEAD_DIM ** -0.5)
    sink_b = sink.astype(jnp.float32).reshape(N_KV_HEADS, Q_GROUP)[None, :, :, None, None]
    (p,) = _sink_softmax([s], sink_b)
    o = jnp.einsum('bhgqk,bkhd->bqhgd', p.astype(v.dtype), v)
    return o.reshape(b, lc, hq * hd)


def _odd_proj(h, w_in):
    cb, cc, cx, cg, q, k, v, ag = _split(
        h @ w_in, [C_WIDTH, C_WIDTH, C_WIDTH, C_WIDTH, ATT_WIDTH, KV_WIDTH, KV_WIDTH, ATT_WIDTH])
    b, l, _ = h.shape
    return (cb, cc, cx, cg,
            q.reshape(b, l, N_Q_HEADS, HEAD_DIM),
            k.reshape(b, l, N_KV_HEADS, HEAD_DIM),
            v.reshape(b, l, N_KV_HEADS, HEAD_DIM), ag)


def _odd_out(cb, cc, cx, cg, att, ag, c_conv_w, w_out):
    y_c = cb * _dwconv(cc * cx, c_conv_w) * jax.nn.silu(cg)
    y_d = att * jax.nn.silu(ag)
    return jnp.concatenate([y_c, y_d], axis=-1) @ w_out


def setup_inputs(seed: int = 0) -> dict:
    key = jax.random.key(seed)
    ks = jax.random.split(key, 24)
    ne, no = (DEPTH + 1) // 2, DEPTH // 2

    def nrm(k, shape):
        return jax.random.normal(k, shape, jnp.float32)

    return {
        'x': nrm(ks[0], (BATCH, SEQ, D_MODEL)),
        'c': nrm(ks[1], (BATCH, D_MODEL)),
        'ctx': nrm(ks[2], (BATCH, CTX_LEN, D_MODEL)),
        'c_ctx': nrm(ks[3], (D_MODEL,)),
        'e_norm': 1.0 + 0.02 * nrm(ks[4], (ne, D_MODEL)),
        'e_w_mod': nrm(ks[5], (ne, D_MODEL, 3 * D_MODEL)) * (0.5 * D_MODEL ** -0.5),
        'e_b_mod': 0.02 * nrm(ks[6], (ne, 3 * D_MODEL)),
        'e_w_in': nrm(ks[7], (ne, D_MODEL, EVEN_IN)) * D_MODEL ** -0.5,
        'e_a_conv_w': nrm(ks[8], (ne, A_CONV_W, A_WIDTH)) * A_CONV_W ** -0.5,
        'e_a_conv_b': 0.02 * nrm(ks[9], (ne, A_WIDTH)),
        'e_a_ln_g': 1.0 + 0.02 * nrm(ks[10], (ne, A_WIDTH)),
        'e_a_ln_b': 0.02 * nrm(ks[11], (ne, A_WIDTH)),
        'e_a_pw_w': nrm(ks[12], (ne, A_WIDTH, A_WIDTH)) * A_WIDTH ** -0.5,
        'e_a_pw_b': 0.02 * nrm(ks[13], (ne, A_WIDTH)),
        'e_b_fw': nrm(ks[14], (ne, B_GROUPS, B_GROUP_DIM, B_GROUP_DIM)) * B_GROUP_DIM ** -0.5,
        'e_w_out': nrm(ks[15], (ne, EVEN_MIX, D_MODEL)) * EVEN_MIX ** -0.5,
        'o_norm': 1.0 + 0.02 * nrm(ks[16], (no, D_MODEL)),
        'o_w_mod': nrm(ks[17], (no, D_MODEL, 3 * D_MODEL)) * (0.5 * D_MODEL ** -0.5),
        'o_b_mod': 0.02 * nrm(ks[18], (no, 3 * D_MODEL)),
        'o_w_in': nrm(ks[19], (no, D_MODEL, ODD_IN)) * D_MODEL ** -0.5,
        'o_c_conv_w': nrm(ks[20], (no, C_CONV_W, C_WIDTH)) * C_CONV_W ** -0.5,
        'o_sink': 0.5 * nrm(ks[21], (no, N_Q_HEADS)),
        'o_w_out': nrm(ks[22], (no, ODD_MIX, D_MODEL)) * ODD_MIX ** -0.5,
        'final_norm': 1.0 + 0.02 * nrm(ks[23], (D_MODEL,)),
    }


def reference(x, c, ctx, c_ctx, e_norm, e_w_mod, e_b_mod, e_w_in, e_a_conv_w, e_a_conv_b,
              e_a_ln_g, e_a_ln_b, e_a_pw_w, e_a_pw_b, e_b_fw, e_w_out,
              o_norm, o_w_mod, o_b_mod, o_w_in, o_c_conv_w, o_sink, o_w_out, final_norm):
    l = x.shape[1]
    rows = l // GRID_W
    row = jnp.broadcast_to(jnp.arange(rows)[:, None], (rows, GRID_W)).reshape(l)
    col = jnp.broadcast_to(jnp.arange(GRID_W)[None, :], (rows, GRID_W)).reshape(l)
    cos, sin = _rope_tables(row, col)
    c_ctx2 = c_ctx[None, :]
    xc = ctx
    for layer in range(DEPTH):
        i = layer // 2
        ctx_needed = layer < DEPTH - 1
        if layer % 2 == 0:
            args = (e_w_in[i], e_a_conv_w[i], e_a_conv_b[i], e_a_ln_g[i], e_a_ln_b[i],
                    e_a_pw_w[i], e_a_pw_b[i], e_b_fw[i], e_w_out[i])
            sh, sc, gt = _modulation(c, e_w_mod[i], e_b_mod[i])
            x_new = x + gt * _even_mix(_pre(x, e_norm[i], sh, sc), *args)
            if ctx_needed:
                sh_c, sc_c, gt_c = _modulation(c_ctx2, e_w_mod[i], e_b_mod[i])
                xc = xc + gt_c * _even_mix(_pre(xc, e_norm[i], sh_c, sc_c), *args)
            x = x_new
        else:
            sh, sc, gt = _modulation(c, o_w_mod[i], o_b_mod[i])
            sh_c, sc_c, gt_c = _modulation(c_ctx2, o_w_mod[i], o_b_mod[i])
            cb_c, cc_c, cx_c, cg_c, q_c, k_c, v_c, ag_c = _odd_proj(_pre(xc, o_norm[i], sh_c, sc_c), o_w_in[i])
            cb, cc, cx, cg, q, k, v, ag = _odd_proj(_pre(x, o_norm[i], sh, sc), o_w_in[i])
            att = _window_attention(_apply_rope(q, cos, sin), _apply_rope(k, cos, sin), v, k_c, v_c, o_sink[i])
            x_new = x + gt * _odd_out(cb, cc, cx, cg, att, ag, o_c_conv_w[i], o_w_out[i])
            if ctx_needed:
                att_c = _context_attention(q_c, k_c, v_c, o_sink[i])
                xc = xc + gt_c * _odd_out(cb_c, cc_c, cx_c, cg_c, att_c, ag_c, o_c_conv_w[i], o_w_out[i])
            x = x_new
    return _rmsnorm(x, final_norm)
```

```python
import functools

import numpy as np
import jax
import jax.numpy as jnp
from jax import lax
from jax.experimental import pallas as pl
from jax.experimental.pallas import tpu as pltpu

D_MODEL = 1024
HALF = 512
A_CONV_W = 31
A_HALO = 16
C_HALO = 8
B_GROUPS = 4
B_GROUP_DIM = 128
HEAD_DIM = 64
N_Q_HEADS = 8
KV_WIDTH = 128
BLOCK = 128
ODD_IN = 3328
EVEN_IN = 2560
ROPE_BASE = 10000.0
NEG_INF = -1e30
EPS = 1e-6
F32 = jnp.float32
BF16 = jnp.bfloat16
VMEM_LIMIT_BYTES = 48 * 1024 * 1024


def _params(*sem):
    return pltpu.CompilerParams(dimension_semantics=sem, vmem_limit_bytes=VMEM_LIMIT_BYTES)


def _dot(a, b):
    return jnp.dot(a, b, preferred_element_type=F32)


def _dot_nt(a, b):
    return lax.dot_general(a, b, (((1,), (1,)), ((), ())), preferred_element_type=F32)


def _pre_norm(x, g_ref, sh_ref, sc_ref):
    mult = g_ref[...] * (1.0 + sc_ref[0])
    ms = jnp.mean(x * x, axis=-1, keepdims=True)
    return (x * lax.rsqrt(ms + EPS)) * mult + sh_ref[0]


def _mod_kernel(c_ref, w_ref, b_ref, o_ref):
    o_ref[0] = _dot(jax.nn.silu(c_ref[...]), w_ref[0]) + b_ref[0]


def _modulation(cv, w_mod, b_mod):
    n = w_mod.shape[0]
    return pl.pallas_call(
        _mod_kernel,
        grid=(n, 3),
        in_specs=[pl.BlockSpec((16, D_MODEL), lambda l, j: (0, 0)),
                  pl.BlockSpec((1, D_MODEL, D_MODEL), lambda l, j: (l, 0, j)),
                  pl.BlockSpec((1, 1, D_MODEL), lambda l, j: (l, 0, j))],
        out_specs=pl.BlockSpec((1, 16, D_MODEL), lambda l, j: (l, 0, j)),
        out_shape=jax.ShapeDtypeStruct((n, 16, 3 * D_MODEL), F32),
        compiler_params=_params("arbitrary", "arbitrary"),
        name="modulation",
    )(cv, w_mod, b_mod.reshape(n, 1, 3 * D_MODEL))


def _mod_specs(nb):
    def spec(j):
        if nb == 1:
            return pl.BlockSpec((1, 1, D_MODEL), lambda b, t: (0, 0, j))
        return pl.BlockSpec((1, 1, D_MODEL), lambda b, t: (b, 0, j))
    return spec(0), spec(1), spec(2)


def _gprep_kernel(cd_ref, sd_ref, fw_ref, o_ref):
    o_ref[...] = jnp.zeros(o_ref.shape, o_ref.dtype)
    for g in range(B_GROUPS):
        fw = fw_ref[0, g]
        gc = jnp.dot(cd_ref[...], fw, preferred_element_type=F32, precision=lax.Precision.HIGHEST)
        gs = jnp.dot(sd_ref[...], fw, preferred_element_type=F32, precision=lax.Precision.HIGHEST)
        lo, hi = g * B_GROUP_DIM, (g + 1) * B_GROUP_DIM
        o_ref[0, lo:hi, lo:hi] = gc.astype(BF16)
        o_ref[0, lo:hi, HALF + lo:HALF + hi] = (-gs).astype(BF16)


def _fourier_channel_mats(fw):
    n = fw.shape[0]
    d = np.arange(B_GROUP_DIM)
    ang = 2.0 * np.pi * ((d[:, None] * d[None, :]) % B_GROUP_DIM) / B_GROUP_DIM
    cd = jnp.asarray(np.cos(ang) / np.sqrt(B_GROUP_DIM), F32)
    sd = jnp.asarray(np.sin(ang) / np.sqrt(B_GROUP_DIM), F32)
    mat = pl.BlockSpec((B_GROUP_DIM, B_GROUP_DIM), lambda i: (0, 0))
    return pl.pallas_call(
        _gprep_kernel,
        grid=(n,),
        in_specs=[mat, mat,
                  pl.BlockSpec((1, B_GROUPS, B_GROUP_DIM, B_GROUP_DIM), lambda i: (i, 0, 0, 0))],
        out_specs=pl.BlockSpec((1, HALF, 2 * HALF), lambda i: (i, 0, 0)),
        out_shape=jax.ShapeDtypeStruct((n, HALF, 2 * HALF), BF16),
        compiler_params=_params("arbitrary"),
        name="fourier_channel_mats",
    )(cd, sd, fw)


def _dft_tables(n1, n2):
    l = n1 * n2
    k1 = np.arange(n1)
    a1 = 2.0 * np.pi * ((k1[:, None] * k1[None, :]) % n1) / n1
    c1, s1 = np.cos(a1), np.sin(a1)
    f1 = np.block([[c1, s1], [-s1, c1]]) / np.sqrt(l)
    l2 = np.arange(n2)
    at = 2.0 * np.pi * (k1[:, None] * l2[None, :]) / l
    tc = np.repeat(np.cos(at), 128, axis=1)
    ts = np.repeat(np.sin(at), 128, axis=1)
    a2 = 2.0 * np.pi * ((l2[:, None] * l2[None, :]) % n2) / n2
    f2 = np.concatenate([np.cos(a2), np.sin(a2)], axis=1)
    return (jnp.asarray(f1, F32).astype(BF16), jnp.asarray(tc, F32), jnp.asarray(ts, F32),
            jnp.asarray(f2, F32).astype(BF16))


def _even_in_kernel(xp_ref, xm_ref, xn_ref, sh_ref, sc_ref, g_ref, win_ref, cw_ref, cb_ref, lg_ref, lb_ref,
                    pw_ref, pb_ref, gm_ref, ya_ref, z_ref, sgb_ref, u_scr, c_scr, *, tl, nt):
    t = pl.program_id(1)
    ext = tl + 2 * A_HALO
    xe = jnp.concatenate([xp_ref[0], xm_ref[0], xn_ref[0]], axis=0)
    he = _pre_norm(xe, g_ref, sh_ref, sc_ref).astype(BF16)
    pa = _dot(he, win_ref[:, 0:2 * HALF])
    u = pa[:, :HALF] * jax.nn.sigmoid(pa[:, HALF:])
    row = lax.broadcasted_iota(jnp.int32, (ext, 1), 0)
    inside = ((row >= A_HALO) | (t > 0)) & ((row < tl + A_HALO) | (t < nt - 1))
    u_scr[...] = jnp.where(inside, u, 0.0)
    pm = _dot(he[A_HALO:A_HALO + tl], win_ref[:, 2 * HALF:EVEN_IN])
    rc = 32
    off = A_HALO - (A_CONV_W - 1) // 2
    for r0 in range(0, tl, rc):
        acc = jnp.broadcast_to(cb_ref[...], (rc, HALF))
        for j in range(A_CONV_W):
            acc = acc + u_scr[pl.ds(r0 + off + j, rc), :] * cw_ref[j:j + 1, :]
        c_scr[r0:r0 + rc, :] = acc
    cv = c_scr[...]
    mu = jnp.mean(cv, axis=-1, keepdims=True)
    cen = cv - mu
    var = jnp.mean(cen * cen, axis=-1, keepdims=True)
    ln = cen * lax.rsqrt(var + EPS) * lg_ref[...] + lb_ref[...]
    ya = (_dot(jax.nn.silu(ln).astype(BF16), pw_ref[...]) + pb_ref[...]) * jax.nn.silu(pm[:, :HALF])
    ya_ref[0] = ya.astype(BF16)
    pq = _dot(pm[:, HALF:2 * HALF].astype(BF16), gm_ref[...])
    z_ref[0, 0] = pq[:, :HALF].astype(BF16)
    z_ref[0, 1] = pq[:, HALF:].astype(BF16)
    sgb_ref[0] = jax.nn.silu(pm[:, 2 * HALF:])


def _fft_stage1_kernel(z_ref, f_ref, tc_ref, ts_ref, o_ref, *, n1, nl2):
    tn = nl2 * HALF
    a = _dot(f_ref[...], z_ref[0].reshape(2 * n1, tn))
    ar, ai = a[:n1], a[n1:]
    for l in range(nl2):
        c = jnp.concatenate([tc_ref[:, l * 128:(l + 1) * 128]] * 4, axis=1)
        s = jnp.concatenate([ts_ref[:, l * 128:(l + 1) * 128]] * 4, axis=1)
        sl = slice(l * HALF, (l + 1) * HALF)
        o_ref[0, 0, :, sl] = (ar[:, sl] * c + ai[:, sl] * s).astype(BF16)
        o_ref[0, 1, :, sl] = (ai[:, sl] * c - ar[:, sl] * s).astype(BF16)


def _even_out_kernel(a_ref, f2_ref, sgb_ref, ya_ref, x_ref, gt_ref, wout_ref, o_ref, *, n2, grp):
    rows = []
    for i in range(grp):
        y = _dot(f2_ref[...], a_ref[0, :, i].reshape(2 * n2, HALF))
        yb = y * sgb_ref[0, :, i * HALF:(i + 1) * HALF]
        rows.append(jnp.concatenate([ya_ref[0, :, i * HALF:(i + 1) * HALF], yb.astype(BF16)], axis=1))
    mix = _dot(jnp.concatenate(rows, axis=0), wout_ref[...])
    gt = gt_ref[0]
    for i in range(grp):
        sl = slice(i * D_MODEL, (i + 1) * D_MODEL)
        o_ref[0, :, sl] = x_ref[0, :, sl] + gt * mix[i * n2:(i + 1) * n2]


def _even_layer(x, mod, norm_g, w_in, conv_w, conv_b, ln_g, ln_b, pw_w, pw_b, gmat, w_out, *, n1, n2, tl, grp, nl2):
    b, l, _ = x.shape
    nb = mod.shape[0]
    nt = l // tl
    hb = tl // A_HALO
    sh, sc, gt = _mod_specs(nb)
    const = lambda shape: pl.BlockSpec(shape, lambda bi, t: (0,) * len(shape))
    ya, z, sgb = pl.pallas_call(
        functools.partial(_even_in_kernel, tl=tl, nt=nt),
        grid=(b, nt),
        in_specs=[pl.BlockSpec((1, A_HALO, D_MODEL), lambda bi, t: (bi, jnp.maximum(t * hb - 1, 0), 0)),
                  pl.BlockSpec((1, tl, D_MODEL), lambda bi, t: (bi, t, 0)),
                  pl.BlockSpec((1, A_HALO, D_MODEL), lambda bi, t: (bi, jnp.minimum((t + 1) * hb, l // A_HALO - 1), 0)),
                  sh, sc, const((1, D_MODEL)), const((D_MODEL, EVEN_IN)), const((A_CONV_W, HALF)),
                  const((1, HALF)), const((1, HALF)), const((1, HALF)), const((HALF, HALF)), const((1, HALF)),
                  const((HALF, 2 * HALF))],
        out_specs=[pl.BlockSpec((1, tl, HALF), lambda bi, t: (bi, t, 0)),
                   pl.BlockSpec((1, 2, tl, HALF), lambda bi, t: (bi, 0, t, 0)),
                   pl.BlockSpec((1, tl, HALF), lambda bi, t: (bi, t, 0))],
        out_shape=[jax.ShapeDtypeStruct((b, l, HALF), BF16),
                   jax.ShapeDtypeStruct((b, 2, l, HALF), BF16),
                   jax.ShapeDtypeStruct((b, l, HALF), F32)],
        scratch_shapes=[pltpu.VMEM((tl + 2 * A_HALO, HALF), F32), pltpu.VMEM((tl, HALF), F32)],
        compiler_params=_params("arbitrary", "arbitrary"),
        name="even_in",
    )(x, x, x, mod, mod, norm_g, w_in, conv_w, conv_b, ln_g, ln_b, pw_w, pw_b, gmat)

    f1, tc, ts, f2 = _dft_tables(n1, n2)
    tn = nl2 * HALF
    a = pl.pallas_call(
        functools.partial(_fft_stage1_kernel, n1=n1, nl2=nl2),
        grid=(b, n2 // nl2),
        in_specs=[pl.BlockSpec((1, 2, n1, tn), lambda bi, j: (bi, 0, 0, j)),
                  pl.BlockSpec((2 * n1, 2 * n1), lambda bi, j: (0, 0)),
                  pl.BlockSpec((n1, nl2 * 128), lambda bi, j: (0, j)),
                  pl.BlockSpec((n1, nl2 * 128), lambda bi, j: (0, j))],
        out_specs=pl.BlockSpec((1, 2, n1, tn), lambda bi, j: (bi, 0, 0, j)),
        out_shape=jax.ShapeDtypeStruct((b, 2, n1, n2 * HALF), BF16),
        compiler_params=_params("arbitrary", "arbitrary"),
        name="fft_stage1",
    )(z.reshape(b, 2, n1, n2 * HALF), f1, tc, ts)

    out = pl.pallas_call(
        functools.partial(_even_out_kernel, n2=n2, grp=grp),
        grid=(b, n1 // grp),
        in_specs=[pl.BlockSpec((1, 2, grp, n2, HALF), lambda bi, j: (bi, 0, j, 0, 0)),
                  pl.BlockSpec((n2, 2 * n2), lambda bi, j: (0, 0)),
                  pl.BlockSpec((1, n2, grp * HALF), lambda bi, j: (bi, 0, j)),
                  pl.BlockSpec((1, n2, grp * HALF), lambda bi, j: (bi, 0, j)),
                  pl.BlockSpec((1, n2, grp * D_MODEL), lambda bi, j: (bi, 0, j)),
                  gt,
                  pl.BlockSpec((D_MODEL, D_MODEL), lambda bi, j: (0, 0))],
        out_specs=pl.BlockSpec((1, n2, grp * D_MODEL), lambda bi, j: (bi, 0, j)),
        out_shape=jax.ShapeDtypeStruct((b, n2, n1 * D_MODEL), F32),
        compiler_params=_params("arbitrary", "arbitrary"),
        name="even_out",
    )(a.reshape(b, 2, n1, n2, HALF), f2, sgb.reshape(b, n2, n1 * HALF), ya.reshape(b, n2, n1 * HALF),
      x.reshape(b, n2, n1 * D_MODEL), mod, w_out)
    return out.reshape(b, l, D_MODEL)


def _rope(t, cos_ref, sin_ref):
    reps = t.shape[1] // 128
    c = jnp.concatenate([cos_ref[...]] * reps, axis=1) if reps > 1 else cos_ref[...]
    s = jnp.concatenate([sin_ref[...]] * reps, axis=1) if reps > 1 else sin_ref[...]
    lane = lax.broadcasted_iota(jnp.int32, t.shape, 1)
    partner = jnp.where((lane & 16) == 0, pltpu.roll(t, t.shape[1] - 16, 1), pltpu.roll(t, 16, 1))
    return t * c + partner * s


def _odd_in_kernel(x_ref, sh_ref, sc_ref, g_ref, win_ref, cos_ref, sin_ref,
                   cbg_ref, z_ref, q_ref, k_ref, v_ref, sag_ref):
    h = _pre_norm(x_ref[0], g_ref, sh_ref, sc_ref).astype(BF16)
    p = _dot(h, win_ref[...])
    cbg_ref[0] = p[:, 0:HALF] * jax.nn.silu(p[:, 3 * HALF:4 * HALF])
    z_ref[0] = p[:, HALF:2 * HALF] * p[:, 2 * HALF:3 * HALF]
    q0 = 4 * HALF
    q_ref[0] = (_rope(p[:, q0:q0 + HALF], cos_ref, sin_ref) * (HEAD_DIM ** -0.5)).astype(BF16)
    k_ref[0] = _rope(p[:, q0 + HALF:q0 + HALF + KV_WIDTH], cos_ref, sin_ref).astype(BF16)
    v_ref[0] = p[:, q0 + HALF + KV_WIDTH:q0 + HALF + 2 * KV_WIDTH].astype(BF16)
    sag_ref[0] = jax.nn.silu(p[:, q0 + HALF + 2 * KV_WIDTH:])


def _odd_in(x, mod, norm_g, w_in, cos_t, sin_t, *, tl):
    b, l, _ = x.shape
    nb = mod.shape[0]
    sh, sc, _ = _mod_specs(nb)
    tile = lambda w: pl.BlockSpec((1, tl, w), lambda bi, t: (bi, t, 0))
    return pl.pallas_call(
        _odd_in_kernel,
        grid=(b, l // tl),
        in_specs=[tile(D_MODEL), sh, sc,
                  pl.BlockSpec((1, D_MODEL), lambda bi, t: (0, 0)),
                  pl.BlockSpec((D_MODEL, ODD_IN), lambda bi, t: (0, 0)),
                  pl.BlockSpec((tl, 128), lambda bi, t: (t, 0)),
                  pl.BlockSpec((tl, 128), lambda bi, t: (t, 0))],
        out_specs=[tile(HALF), tile(HALF), tile(HALF), tile(KV_WIDTH), tile(KV_WIDTH), tile(HALF)],
        out_shape=[jax.ShapeDtypeStruct((b, l, HALF), F32), jax.ShapeDtypeStruct((b, l, HALF), F32),
                   jax.ShapeDtypeStruct((b, l, HALF), BF16), jax.ShapeDtypeStruct((b, l, KV_WIDTH), BF16),
                   jax.ShapeDtypeStruct((b, l, KV_WIDTH), BF16), jax.ShapeDtypeStruct((b, l, HALF), F32)],
        compiler_params=_params("arbitrary", "arbitrary"),
        name="odd_in",
    )(x, mod, mod, norm_g, w_in, cos_t, sin_t)


def _odd_out_kernel(*refs, tq, nt, window, final):
    refs = list(refs)
    q_ref = refs.pop(0)
    if window:
        kp_ref, km_ref, kn_ref, vp_ref, vm_ref, vn_ref = refs[:6]
        refs = refs[6:]
    kc_ref, vc_ref, sink_ref, zp_ref, zm_ref, zn_ref, cbg_ref, sag_ref, x_ref, gt_ref, cw_ref, wout_ref = refs[:12]
    refs = refs[12:]
    fn_ref = refs.pop(0) if final else None
    o_ref = refs.pop(0)

    t = pl.program_id(1)
    nqb = tq // BLOCK
    lane = lax.broadcasted_iota(jnp.int32, (BLOCK, 128), 1)
    low = lane < HEAD_DIM
    kc, vc = kc_ref[0], vc_ref[0]
    if window:
        kext = jnp.concatenate([kp_ref[0], km_ref[0], kn_ref[0]], axis=0)
        vext = jnp.concatenate([vp_ref[0], vm_ref[0], vn_ref[0]], axis=0)
        qi = lax.broadcasted_iota(jnp.int32, (BLOCK, 3 * BLOCK), 0)
        kj = lax.broadcasted_iota(jnp.int32, (BLOCK, 3 * BLOCK), 1)
        band = (kj - qi >= 0) & (kj - qi <= 2 * BLOCK)

    att_rows = []
    for qb in range(nqb):
        qg = q_ref[0, qb * BLOCK:(qb + 1) * BLOCK, :]
        zero = jnp.zeros((BLOCK, 128), BF16)
        slots = []
        for g in range(4):
            qq = qg[:, g * 128:(g + 1) * 128]
            slots += [jnp.where(low, qq, zero), jnp.where(low, zero, qq)]
        qs = jnp.concatenate(slots, axis=0)
        s_c = _dot_nt(qs, kc)
        if window:
            s_w = _dot_nt(qs, kext[qb * BLOCK:(qb + 3) * BLOCK])
            valid = band
            if qb == 0:
                valid = valid & ((kj >= BLOCK) | (t > 0))
            if qb == nqb - 1:
                valid = valid & ((kj < 2 * BLOCK) | (t < nt - 1))
        pws, pcs, dens = [], [], []
        for s in range(8):
            sink = sink_ref[s]
            sc_s = s_c[s * BLOCK:(s + 1) * BLOCK]
            m = jnp.maximum(jnp.max(sc_s, axis=-1, keepdims=True), sink)
            if window:
                sw_s = jnp.where(valid, s_w[s * BLOCK:(s + 1) * BLOCK], NEG_INF)
                m = jnp.maximum(m, jnp.max(sw_s, axis=-1, keepdims=True))
            pc = jnp.exp(sc_s - m)
            den = jnp.exp(sink - m) + jnp.sum(pc, axis=-1, keepdims=True)
            if window:
                pw = jnp.exp(sw_s - m)
                den = den + jnp.sum(pw, axis=-1, keepdims=True)
                pws.append(pw.astype(BF16))
            pcs.append(pc.astype(BF16))
            dens.append(den)
        o = _dot(jnp.concatenate(pcs, axis=0), vc)
        if window:
            o = o + _dot(jnp.concatenate(pws, axis=0), vext[qb * BLOCK:(qb + 3) * BLOCK])
        o = o / jnp.concatenate(dens, axis=0)
        att_rows.append(jnp.concatenate(
            [jnp.where(low, o[(2 * g) * BLOCK:(2 * g + 1) * BLOCK], o[(2 * g + 1) * BLOCK:(2 * g + 2) * BLOCK])
             for g in range(4)], axis=1))
    att = jnp.concatenate(att_rows, axis=0) if nqb > 1 else att_rows[0]
    yd = att * sag_ref[0]

    zp = jnp.where(t > 0, zp_ref[0], 0.0)
    zn = jnp.where(t < nt - 1, zn_ref[0], 0.0)
    zext = jnp.concatenate([zp, zm_ref[0], zn], axis=0)
    conv = (zext[C_HALO - 1:C_HALO - 1 + tq] * cw_ref[0:1, :] + zext[C_HALO:C_HALO + tq] * cw_ref[1:2, :]
            + zext[C_HALO + 1:C_HALO + 1 + tq] * cw_ref[2:3, :])
    yc = cbg_ref[0] * conv
    mix = _dot(jnp.concatenate([yc.astype(BF16), yd.astype(BF16)], axis=1), wout_ref[...])
    out = x_ref[0] + gt_ref[0] * mix
    if final:
        ms = jnp.mean(out * out, axis=-1, keepdims=True)
        out = out * lax.rsqrt(ms + EPS) * fn_ref[...]
    o_ref[0] = out


def _odd_out(x, mod, q, k, v, kc, vc, sink, z, cbg, sag, conv_w, w_out, final_norm, *, tq, window):
    b, l, _ = x.shape
    nb = mod.shape[0]
    nt = l // tq
    nqb = tq // BLOCK
    lc = kc.shape[1]
    _, _, gt = _mod_specs(nb)
    tile = lambda w: pl.BlockSpec((1, tq, w), lambda bi, t: (bi, t, 0))
    prev_blk = pl.BlockSpec((1, BLOCK, KV_WIDTH), lambda bi, t: (bi, jnp.maximum(t * nqb - 1, 0), 0))
    next_blk = pl.BlockSpec((1, BLOCK, KV_WIDTH), lambda bi, t: (bi, jnp.minimum((t + 1) * nqb, l // BLOCK - 1), 0))
    zh = tq // C_HALO
    args, specs = [q], [tile(HALF)]
    if window:
        args += [k, k, k, v, v, v]
        specs += [prev_blk, tile(KV_WIDTH), next_blk, prev_blk, tile(KV_WIDTH), next_blk]
    args += [kc, vc, sink, z, z, z, cbg, sag, x, mod, conv_w, w_out]
    specs += [pl.BlockSpec((1, lc, KV_WIDTH), lambda bi, t: (bi, 0, 0)),
              pl.BlockSpec((1, lc, KV_WIDTH), lambda bi, t: (bi, 0, 0)),
              pl.BlockSpec(memory_space=pltpu.SMEM),
              pl.BlockSpec((1, C_HALO, HALF), lambda bi, t: (bi, jnp.maximum(t * zh - 1, 0), 0)),
              tile(HALF),
              pl.BlockSpec((1, C_HALO, HALF), lambda bi, t: (bi, jnp.minimum((t + 1) * zh, l // C_HALO - 1), 0)),
              tile(HALF), tile(HALF), tile(D_MODEL), gt,
              pl.BlockSpec((3, HALF), lambda bi, t: (0, 0)),
              pl.BlockSpec((D_MODEL, D_MODEL), lambda bi, t: (0, 0))]
    if final_norm is not None:
        args.append(final_norm)
        specs.append(pl.BlockSpec((1, D_MODEL), lambda bi, t: (0, 0)))
    return pl.pallas_call(
        functools.partial(_odd_out_kernel, tq=tq, nt=nt, window=window, final=final_norm is not None),
        grid=(b, nt),
        in_specs=specs,
        out_specs=tile(D_MODEL),
        out_shape=jax.ShapeDtypeStruct((b, l, D_MODEL), F32),
        compiler_params=_params("arbitrary", "arbitrary"),
        name="odd_out_window" if window else "odd_out_context",
    )(*args)


def _pair_heads_cols(w):
    lead = w.shape[:-1]
    return w.reshape(*lead, 2, 4, HEAD_DIM).swapaxes(-3, -2).reshape(*lead, HALF)


def _rope_tables(l, grid_w):
    nf = HEAD_DIM // 4
    pos = jnp.arange(l)
    inv = jnp.power(ROPE_BASE, -jnp.arange(nf, dtype=F32) / nf)
    ang = jnp.stack([(pos // grid_w).astype(F32)[:, None] * inv, (pos % grid_w).astype(F32)[:, None] * inv], axis=1)
    cos, sin = jnp.cos(ang), jnp.sin(ang)
    cos_t = jnp.stack([cos, cos], axis=2).reshape(l, HEAD_DIM)
    sin_t = jnp.stack([-sin, sin], axis=2).reshape(l, HEAD_DIM)
    return jnp.tile(cos_t, (1, 2)), jnp.tile(sin_t, (1, 2))


X_SHAPE = dict(n1=64, n2=64, tl=512, grp=8, nl2=8)
C_SHAPE = dict(n1=16, n2=16, tl=256, grp=16, nl2=16)
GRID_W = 64


def kernel(x, c, ctx, c_ctx, e_norm, e_w_mod, e_b_mod, e_w_in, e_a_conv_w, e_a_conv_b, e_a_ln_g, e_a_ln_b,
           e_a_pw_w, e_a_pw_b, e_b_fw, e_w_out, o_norm, o_w_mod, o_b_mod, o_w_in, o_c_conv_w, o_sink, o_w_out,
           final_norm):
    b, l, _ = x.shape
    lc = ctx.shape[1]
    assert x.shape == (8, 4096, D_MODEL) and ctx.shape == (8, 256, D_MODEL)
    depth = e_norm.shape[0] + o_norm.shape[0]

    cv = jnp.zeros((16, D_MODEL), F32).at[:b].set(c).at[b].set(c_ctx)
    mod_e = _modulation(cv, e_w_mod, e_b_mod)
    mod_o = _modulation(cv, o_w_mod, o_b_mod)
    gmat = _fourier_channel_mats(e_b_fw)

    cos_x, sin_x = _rope_tables(l, GRID_W)
    cos_c, sin_c = jnp.ones((lc, 128), F32), jnp.zeros((lc, 128), F32)

    row = lambda a: a.reshape(1, -1)
    xc = ctx
    for layer in range(depth):
        i = layer // 2
        ctx_needed = layer < depth - 1
        if layer % 2 == 0:
            mx, mc = mod_e[i, :b].reshape(b, 1, -1), mod_e[i, b:b + 1].reshape(1, 1, -1)
            wts = (row(e_norm[i]), e_w_in[i].astype(BF16), e_a_conv_w[i], row(e_a_conv_b[i]), row(e_a_ln_g[i]),
                   row(e_a_ln_b[i]), e_a_pw_w[i].astype(BF16), row(e_a_pw_b[i]), gmat[i], e_w_out[i].astype(BF16))
            x = _even_layer(x, mx, *wts, **X_SHAPE)
            if ctx_needed:
                xc = _even_layer(xc, mc, *wts, **C_SHAPE)
        else:
            mx, mc = mod_o[i, :b].reshape(b, 1, -1), mod_o[i, b:b + 1].reshape(1, 1, -1)
            w = o_w_in[i]
            w_in = jnp.concatenate([w[:, :4 * HALF], _pair_heads_cols(w[:, 4 * HALF:5 * HALF]),
                                    w[:, 5 * HALF:5 * HALF + 2 * KV_WIDTH],
                                    _pair_heads_cols(w[:, 5 * HALF + 2 * KV_WIDTH:])], axis=1).astype(BF16)
            wo = o_w_out[i]
            wo_att = wo[HALF:].reshape(2, 4, HEAD_DIM, D_MODEL).swapaxes(0, 1).reshape(HALF, D_MODEL)
            w_out = jnp.concatenate([wo[:HALF], wo_att], axis=0).astype(BF16)
            sink = o_sink[i].reshape(2, 4).T.reshape(N_Q_HEADS)
            norm_g = row(o_norm[i])
            cbg_c, z_c, q_c, k_c, v_c, sag_c = _odd_in(xc, mc, norm_g, w_in, cos_c, sin_c, tl=lc)
            cbg, z, q, k, v, sag = _odd_in(x, mx, norm_g, w_in, cos_x, sin_x, tl=512)
            last = layer == depth - 1
            x = _odd_out(x, mx, q, k, v, k_c, v_c, sink, z, cbg, sag, o_c_conv_w[i], w_out,
                         row(final_norm) if last else None, tq=512, window=True)
            if ctx_needed:
                xc = _odd_out(xc, mc, q_c, None, None, k_c, v_c, sink, z_c, cbg_c, sag_c, o_c_conv_w[i], w_out,
                              None, tq=lc, window=False)
    return x
```

```python
import functools

import numpy as np
import jax
import jax.numpy as jnp
from jax import lax
from jax.experimental import pallas as pl
from jax.experimental.pallas import tpu as pltpu

D_MODEL = 1024
HALF = 512
A_CONV_W = 31
A_HALO = 16
C_HALO = 8
SUBLANES = 8
B_GROUPS = 4
B_GROUP_DIM = 128
HEAD_DIM = 64
N_Q_HEADS = 8
KV_WIDTH = 128
BLOCK = 128
ODD_IN = 3328
EVEN_IN = 2560
ROPE_BASE = 10000.0
NEG_INF = -1e30
EPS = 1e-6
RADIX = 16
FFT_COLS = 256
F32 = jnp.float32
BF16 = jnp.bfloat16
VMEM_LIMIT_BYTES = 48 * 1024 * 1024


def _params(*sem):
    return pltpu.CompilerParams(dimension_semantics=sem, vmem_limit_bytes=VMEM_LIMIT_BYTES)


def _dot(a, b):
    return jnp.dot(a, b, preferred_element_type=F32)


def _dot_nt(a, b):
    return lax.dot_general(a, b, (((1,), (1,)), ((), ())), preferred_element_type=F32)


def _pre_norm(x, g_ref, sh_ref, sc_ref):
    mult = g_ref[...] * (1.0 + sc_ref[0])
    ms = jnp.mean(x * x, axis=-1, keepdims=True)
    return (x * lax.rsqrt(ms + EPS)) * mult + sh_ref[0]


def _mod_kernel(c_ref, w_ref, b_ref, o_ref):
    o_ref[0] = _dot(jax.nn.silu(c_ref[...]), w_ref[0]) + b_ref[0]


def _modulation(cv, w_mod, b_mod):
    n = w_mod.shape[0]
    return pl.pallas_call(
        _mod_kernel,
        grid=(n, 3),
        in_specs=[pl.BlockSpec((16, D_MODEL), lambda l, j: (0, 0)),
                  pl.BlockSpec((1, D_MODEL, D_MODEL), lambda l, j: (l, 0, j)),
                  pl.BlockSpec((1, 1, D_MODEL), lambda l, j: (l, 0, j))],
        out_specs=pl.BlockSpec((1, 16, D_MODEL), lambda l, j: (l, 0, j)),
        out_shape=jax.ShapeDtypeStruct((n, 16, 3 * D_MODEL), F32),
        compiler_params=_params("arbitrary", "arbitrary"),
        name="modulation",
    )(cv, w_mod, b_mod.reshape(n, 1, 3 * D_MODEL))


def _mod_specs(nb):
    def spec(j):
        if nb == 1:
            return pl.BlockSpec((1, 1, D_MODEL), lambda b, t: (0, 0, j))
        return pl.BlockSpec((1, 1, D_MODEL), lambda b, t: (b, 0, j))
    return spec(0), spec(1), spec(2)


def _gprep_kernel(cd_ref, sd_ref, fw_ref, o_ref):
    o_ref[...] = jnp.zeros(o_ref.shape, o_ref.dtype)
    for g in range(B_GROUPS):
        fw = fw_ref[0, g]
        gc = jnp.dot(cd_ref[...], fw, preferred_element_type=F32, precision=lax.Precision.HIGHEST)
        gs = jnp.dot(sd_ref[...], fw, preferred_element_type=F32, precision=lax.Precision.HIGHEST)
        lo, hi = g * B_GROUP_DIM, (g + 1) * B_GROUP_DIM
        o_ref[0, lo:hi, lo:hi] = gc.astype(BF16)
        o_ref[0, lo:hi, HALF + lo:HALF + hi] = (-gs).astype(BF16)


def _fourier_channel_mats(fw):
    n = fw.shape[0]
    d = np.arange(B_GROUP_DIM)
    ang = 2.0 * np.pi * ((d[:, None] * d[None, :]) % B_GROUP_DIM) / B_GROUP_DIM
    cd = jnp.asarray(np.cos(ang) / np.sqrt(B_GROUP_DIM), F32)
    sd = jnp.asarray(np.sin(ang) / np.sqrt(B_GROUP_DIM), F32)
    mat = pl.BlockSpec((B_GROUP_DIM, B_GROUP_DIM), lambda i: (0, 0))
    return pl.pallas_call(
        _gprep_kernel,
        grid=(n,),
        in_specs=[mat, mat,
                  pl.BlockSpec((1, B_GROUPS, B_GROUP_DIM, B_GROUP_DIM), lambda i: (i, 0, 0, 0))],
        out_specs=pl.BlockSpec((1, HALF, 2 * HALF), lambda i: (i, 0, 0)),
        out_shape=jax.ShapeDtypeStruct((n, HALF, 2 * HALF), BF16),
        compiler_params=_params("arbitrary"),
        name="fourier_channel_mats",
    )(cd, sd, fw)


def _real_stack(m, real_only=False):
    top = np.concatenate([m.real, -m.imag], axis=1)
    if real_only:
        return top
    return np.concatenate([top, np.concatenate([m.imag, m.real], axis=1)], axis=0)


def _fft_tables_4096():
    r = np.arange(RADIX)
    eye = np.eye(RADIX)
    w16 = np.exp(-2j * np.pi * np.outer(r, r) / 16)
    w256 = np.exp(-2j * np.pi * np.outer(r, r) / 256)
    w4096 = np.exp(-2j * np.pi * np.outer(r, r) / 4096)
    ma = np.einsum("ka,kc,cd->kcad", w16, w4096, eye).reshape(256, 256) / 64.0
    mb = np.einsum("kb,kc,cd->kcbd", w16, w256, eye).reshape(256, 256)
    mc = np.einsum("kc,ae->kaec", w16, eye).reshape(256, 256)
    ph = 2.0 * np.pi * np.outer(r, r) / 256
    as_bf16 = lambda m: jnp.asarray(m, F32).astype(BF16)
    return (as_bf16(_real_stack(ma)), as_bf16(_real_stack(mb)), as_bf16(_real_stack(mc, real_only=True)),
            jnp.asarray(np.cos(ph), F32), jnp.asarray(np.sin(ph), F32))


def _fft_4096_kernel(z_ref, ma_ref, mb_ref, mc_ref, tc_ref, ts_ref, o_ref, s_scr):
    ct = z_ref.shape[-1]
    blk = RADIX * RADIX
    for b in range(RADIX):
        a1 = _dot(ma_ref[...], z_ref[0, :, :, b].reshape(2 * blk, ct))
        ar, ai = a1[:blk], a1[blk:]
        out_r, out_i = [], []
        for ka in range(RADIX):
            rr, ii = ar[ka * RADIX:(ka + 1) * RADIX], ai[ka * RADIX:(ka + 1) * RADIX]
            if b * ka == 0:
                out_r.append(rr)
                out_i.append(ii)
            else:
                c, s = tc_ref[b, ka], ts_ref[b, ka]
                out_r.append(rr * c + ii * s)
                out_i.append(ii * c - rr * s)
        s_scr[0, :, b] = jnp.concatenate(out_r, axis=0).astype(BF16).reshape(RADIX, RADIX, ct)
        s_scr[1, :, b] = jnp.concatenate(out_i, axis=0).astype(BF16).reshape(RADIX, RADIX, ct)
    for ka in range(RADIX):
        a2 = _dot(mb_ref[...], s_scr[:, ka].reshape(2 * blk, ct))
        s_scr[:, ka] = a2.astype(BF16).reshape(2, RADIX, RADIX, ct)
    for kb in range(RADIX):
        y = _dot(mc_ref[...], s_scr[:, :, kb].reshape(2 * blk, ct))
        o_ref[0, :, kb] = y.reshape(RADIX, RADIX, ct)


def _fft_4096(z):
    b = z.shape[0]
    ma, mb, mc, tc, ts = _fft_tables_4096()
    blk = RADIX * RADIX
    mat = lambda rows: pl.BlockSpec((rows, 2 * blk), lambda bi, j: (0, 0))
    smem = pl.BlockSpec(memory_space=pltpu.SMEM)
    y = pl.pallas_call(
        _fft_4096_kernel,
        grid=(b, HALF // FFT_COLS),
        in_specs=[pl.BlockSpec((1, 2, RADIX, RADIX, RADIX, FFT_COLS), lambda bi, j: (bi, 0, 0, 0, 0, j)),
                  mat(2 * blk), mat(2 * blk), mat(blk), smem, smem],
        out_specs=pl.BlockSpec((1, RADIX, RADIX, RADIX, FFT_COLS), lambda bi, j: (bi, 0, 0, 0, j)),
        out_shape=jax.ShapeDtypeStruct((b, RADIX, RADIX, RADIX, HALF), F32),
        scratch_shapes=[pltpu.VMEM((2, RADIX, RADIX, RADIX, FFT_COLS), BF16)],
        compiler_params=_params("arbitrary", "arbitrary"),
        name="fft_4096",
    )(z.reshape(b, 2, RADIX, RADIX, RADIX, HALF), ma, mb, mc, tc, ts)
    return y.reshape(b, RADIX ** 3, HALF)


def _fft_dense_kernel(z_ref, f_ref, o_ref):
    o_ref[0] = _dot(f_ref[...], z_ref[0].reshape(f_ref.shape[1], z_ref.shape[-1]))


def _fft_dense(z):
    b, _, l, _ = z.shape
    k = np.arange(l)
    ang = 2.0 * np.pi * ((k[:, None] * k[None, :]) % l) / l
    f = jnp.asarray(np.concatenate([np.cos(ang), np.sin(ang)], axis=1) / np.sqrt(l), F32).astype(BF16)
    return pl.pallas_call(
        _fft_dense_kernel,
        grid=(b, HALF // FFT_COLS),
        in_specs=[pl.BlockSpec((1, 2, l, FFT_COLS), lambda bi, j: (bi, 0, 0, j)),
                  pl.BlockSpec((l, 2 * l), lambda bi, j: (0, 0))],
        out_specs=pl.BlockSpec((1, l, FFT_COLS), lambda bi, j: (bi, 0, j)),
        out_shape=jax.ShapeDtypeStruct((b, l, HALF), F32),
        compiler_params=_params("arbitrary", "arbitrary"),
        name="fft_dense",
    )(z, f)


def _even_in_kernel(xp_ref, xm_ref, xn_ref, sh_ref, sc_ref, g_ref, win_ref, cw_ref, cb_ref, lg_ref, lb_ref,
                    pw_ref, pb_ref, gm_ref, ya_ref, z_ref, sgb_ref, u_scr, c_scr, *, tl, nt):
    t = pl.program_id(1)
    ext = tl + 2 * A_HALO
    xe = jnp.concatenate([xp_ref[0], xm_ref[0], xn_ref[0]], axis=0)
    he = _pre_norm(xe, g_ref, sh_ref, sc_ref).astype(BF16)
    pa = _dot(he, win_ref[:, 0:2 * HALF])
    u = pa[:, :HALF] * jax.nn.sigmoid(pa[:, HALF:])
    row = lax.broadcasted_iota(jnp.int32, (ext, 1), 0)
    inside = ((row >= A_HALO) | (t > 0)) & ((row < tl + A_HALO) | (t < nt - 1))
    u = jnp.where(inside, u, 0.0)
    u_scr[0] = u
    u3 = u.reshape(ext // SUBLANES, SUBLANES, HALF)
    sub = lax.broadcasted_iota(jnp.int32, u3.shape, 1)
    for r in range(1, SUBLANES):
        rot = pltpu.roll(u3, SUBLANES - r, 1)
        nxt = jnp.concatenate([rot[1:], rot[:1]], axis=0)
        u_scr[r] = jnp.where(sub < SUBLANES - r, rot, nxt).reshape(ext, HALF)
    pm = _dot(he[A_HALO:A_HALO + tl], win_ref[:, 2 * HALF:EVEN_IN])
    rc = 32
    off = A_HALO - (A_CONV_W - 1) // 2
    for r0 in range(0, tl, rc):
        acc = jnp.broadcast_to(cb_ref[...][None], (rc // SUBLANES, SUBLANES, HALF))
        for j in range(A_CONV_W):
            q, r = divmod(off + j, SUBLANES)
            uj = u_scr[r, r0 + SUBLANES * q:r0 + SUBLANES * q + rc, :]
            acc = acc + uj.reshape(rc // SUBLANES, SUBLANES, HALF) * cw_ref[j][None]
        c_scr[r0:r0 + rc, :] = acc.reshape(rc, HALF)
    cv = c_scr[...]
    mu = jnp.mean(cv, axis=-1, keepdims=True)
    cen = cv - mu
    var = jnp.mean(cen * cen, axis=-1, keepdims=True)
    ln = cen * lax.rsqrt(var + EPS) * lg_ref[...] + lb_ref[...]
    ya = (_dot(jax.nn.silu(ln).astype(BF16), pw_ref[...]) + pb_ref[...]) * jax.nn.silu(pm[:, :HALF])
    ya_ref[0] = ya.astype(BF16)
    pq = _dot(pm[:, HALF:2 * HALF].astype(BF16), gm_ref[...])
    z_ref[0, 0] = pq[:, :HALF].astype(BF16)
    z_ref[0, 1] = pq[:, HALF:].astype(BF16)
    sgb_ref[0] = jax.nn.silu(pm[:, 2 * HALF:])


def _even_out_kernel(y_ref, sgb_ref, ya_ref, x_ref, gt_ref, wout_ref, o_ref):
    yb = (y_ref[0] * sgb_ref[0]).astype(BF16)
    mix = _dot(jnp.concatenate([ya_ref[0], yb], axis=1), wout_ref[...])
    o_ref[0] = x_ref[0] + gt_ref[0] * mix


def _even_layer(x, mod, norm_g, w_in, conv_w, conv_b, ln_g, ln_b, pw_w, pw_b, gmat, w_out, *, tl):
    b, l, _ = x.shape
    nb = mod.shape[0]
    nt = l // tl
    hb = tl // A_HALO
    sh, sc, gt = _mod_specs(nb)
    const = lambda shape: pl.BlockSpec(shape, lambda bi, t: (0,) * len(shape))
    tile = lambda w: pl.BlockSpec((1, tl, w), lambda bi, t: (bi, t, 0))
    ya, z, sgb = pl.pallas_call(
        functools.partial(_even_in_kernel, tl=tl, nt=nt),
        grid=(b, nt),
        in_specs=[pl.BlockSpec((1, A_HALO, D_MODEL), lambda bi, t: (bi, jnp.maximum(t * hb - 1, 0), 0)),
                  tile(D_MODEL),
                  pl.BlockSpec((1, A_HALO, D_MODEL), lambda bi, t: (bi, jnp.minimum((t + 1) * hb, l // A_HALO - 1), 0)),
                  sh, sc, const((1, D_MODEL)), const((D_MODEL, EVEN_IN)), const((A_CONV_W, SUBLANES, HALF)),
                  const((SUBLANES, HALF)), const((1, HALF)), const((1, HALF)), const((HALF, HALF)), const((1, HALF)),
                  const((HALF, 2 * HALF))],
        out_specs=[tile(HALF),
                   pl.BlockSpec((1, 2, tl, HALF), lambda bi, t: (bi, 0, t, 0)),
                   tile(HALF)],
        out_shape=[jax.ShapeDtypeStruct((b, l, HALF), BF16),
                   jax.ShapeDtypeStruct((b, 2, l, HALF), BF16),
                   jax.ShapeDtypeStruct((b, l, HALF), F32)],
        scratch_shapes=[pltpu.VMEM((SUBLANES, tl + 2 * A_HALO, HALF), F32), pltpu.VMEM((tl, HALF), F32)],
        compiler_params=_params("arbitrary", "arbitrary"),
        name="even_in",
    )(x, x, x, mod, mod, norm_g, w_in, conv_w, conv_b, ln_g, ln_b, pw_w, pw_b, gmat)

    y = _fft_4096(z) if l == RADIX ** 3 else _fft_dense(z)

    return pl.pallas_call(
        _even_out_kernel,
        grid=(b, nt),
        in_specs=[tile(HALF), tile(HALF), tile(HALF), tile(D_MODEL), gt, const((D_MODEL, D_MODEL))],
        out_specs=tile(D_MODEL),
        out_shape=jax.ShapeDtypeStruct((b, l, D_MODEL), F32),
        compiler_params=_params("arbitrary", "arbitrary"),
        name="even_out",
    )(y, sgb, ya, x, mod, w_out)


def _rope(t, cos_ref, sin_ref):
    reps = t.shape[1] // 128
    c = jnp.concatenate([cos_ref[...]] * reps, axis=1) if reps > 1 else cos_ref[...]
    s = jnp.concatenate([sin_ref[...]] * reps, axis=1) if reps > 1 else sin_ref[...]
    lane = lax.broadcasted_iota(jnp.int32, t.shape, 1)
    partner = jnp.where((lane & 16) == 0, pltpu.roll(t, t.shape[1] - 16, 1), pltpu.roll(t, 16, 1))
    return t * c + partner * s


def _odd_in_kernel(x_ref, sh_ref, sc_ref, g_ref, win_ref, cos_ref, sin_ref,
                   cbg_ref, z_ref, q_ref, k_ref, v_ref, sag_ref):
    h = _pre_norm(x_ref[0], g_ref, sh_ref, sc_ref).astype(BF16)
    p = _dot(h, win_ref[...])
    cbg_ref[0] = p[:, 0:HALF] * jax.nn.silu(p[:, 3 * HALF:4 * HALF])
    z_ref[0] = p[:, HALF:2 * HALF] * p[:, 2 * HALF:3 * HALF]
    q0 = 4 * HALF
    q_ref[0] = (_rope(p[:, q0:q0 + HALF], cos_ref, sin_ref) * (HEAD_DIM ** -0.5)).astype(BF16)
    k_ref[0] = _rope(p[:, q0 + HALF:q0 + HALF + KV_WIDTH], cos_ref, sin_ref).astype(BF16)
    v_ref[0] = p[:, q0 + HALF + KV_WIDTH:q0 + HALF + 2 * KV_WIDTH].astype(BF16)
    sag_ref[0] = jax.nn.silu(p[:, q0 + HALF + 2 * KV_WIDTH:])


def _odd_in(x, mod, norm_g, w_in, cos_t, sin_t, *, tl):
    b, l, _ = x.shape
    nb = mod.shape[0]
    sh, sc, _ = _mod_specs(nb)
    tile = lambda w: pl.BlockSpec((1, tl, w), lambda bi, t: (bi, t, 0))
    return pl.pallas_call(
        _odd_in_kernel,
        grid=(b, l // tl),
        in_specs=[tile(D_MODEL), sh, sc,
                  pl.BlockSpec((1, D_MODEL), lambda bi, t: (0, 0)),
                  pl.BlockSpec((D_MODEL, ODD_IN), lambda bi, t: (0, 0)),
                  pl.BlockSpec((tl, 128), lambda bi, t: (t, 0)),
                  pl.BlockSpec((tl, 128), lambda bi, t: (t, 0))],
        out_specs=[tile(HALF), tile(HALF), tile(HALF), tile(KV_WIDTH), tile(KV_WIDTH), tile(HALF)],
        out_shape=[jax.ShapeDtypeStruct((b, l, HALF), F32), jax.ShapeDtypeStruct((b, l, HALF), F32),
                   jax.ShapeDtypeStruct((b, l, HALF), BF16), jax.ShapeDtypeStruct((b, l, KV_WIDTH), BF16),
                   jax.ShapeDtypeStruct((b, l, KV_WIDTH), BF16), jax.ShapeDtypeStruct((b, l, HALF), F32)],
        compiler_params=_params("arbitrary", "arbitrary"),
        name="odd_in",
    )(x, mod, mod, norm_g, w_in, cos_t, sin_t)


def _odd_out_kernel(*refs, tq, nt, window, final):
    refs = list(refs)
    q_ref = refs.pop(0)
    if window:
        kp_ref, km_ref, kn_ref, vp_ref, vm_ref, vn_ref = refs[:6]
        refs = refs[6:]
    kc_ref, vc_ref, sink_ref, zp_ref, zm_ref, zn_ref, cbg_ref, sag_ref, x_ref, gt_ref, cw_ref, wout_ref = refs[:12]
    refs = refs[12:]
    fn_ref = refs.pop(0) if final else None
    o_ref = refs.pop(0)

    t = pl.program_id(1)
    nqb = tq // BLOCK
    lane = lax.broadcasted_iota(jnp.int32, (BLOCK, 128), 1)
    low = lane < HEAD_DIM
    kc, vc = kc_ref[0], vc_ref[0]
    if window:
        kext = jnp.concatenate([kp_ref[0], km_ref[0], kn_ref[0]], axis=0)
        vext = jnp.concatenate([vp_ref[0], vm_ref[0], vn_ref[0]], axis=0)
        qi = lax.broadcasted_iota(jnp.int32, (BLOCK, 3 * BLOCK), 0)
        kj = lax.broadcasted_iota(jnp.int32, (BLOCK, 3 * BLOCK), 1)
        band = (kj - qi >= 0) & (kj - qi <= 2 * BLOCK)

    att_rows = []
    for qb in range(nqb):
        qg = q_ref[0, qb * BLOCK:(qb + 1) * BLOCK, :]
        zero = jnp.zeros((BLOCK, 128), BF16)
        slots = []
        for g in range(4):
            qq = qg[:, g * 128:(g + 1) * 128]
            slots += [jnp.where(low, qq, zero), jnp.where(low, zero, qq)]
        qs = jnp.concatenate(slots, axis=0)
        s_c = _dot_nt(qs, kc)
        if window:
            s_w = _dot_nt(qs, kext[qb * BLOCK:(qb + 3) * BLOCK])
            valid = band
            if qb == 0:
                valid = valid & ((kj >= BLOCK) | (t > 0))
            if qb == nqb - 1:
                valid = valid & ((kj < 2 * BLOCK) | (t < nt - 1))
        pws, pcs, dens = [], [], []
        for s in range(8):
            sink = sink_ref[s]
            sc_s = s_c[s * BLOCK:(s + 1) * BLOCK]
            m = jnp.maximum(jnp.max(sc_s, axis=-1, keepdims=True), sink)
            if window:
                sw_s = jnp.where(valid, s_w[s * BLOCK:(s + 1) * BLOCK], NEG_INF)
                m = jnp.maximum(m, jnp.max(sw_s, axis=-1, keepdims=True))
            pc = jnp.exp(sc_s - m)
            den = jnp.exp(sink - m) + jnp.sum(pc, axis=-1, keepdims=True)
            if window:
                pw = jnp.exp(sw_s - m)
                den = den + jnp.sum(pw, axis=-1, keepdims=True)
                pws.append(pw.astype(BF16))
            pcs.append(pc.astype(BF16))
            dens.append(den)
        o = _dot(jnp.concatenate(pcs, axis=0), vc)
        if window:
            o = o + _dot(jnp.concatenate(pws, axis=0), vext[qb * BLOCK:(qb + 3) * BLOCK])
        o = o / jnp.concatenate(dens, axis=0)
        att_rows.append(jnp.concatenate(
            [jnp.where(low, o[(2 * g) * BLOCK:(2 * g + 1) * BLOCK], o[(2 * g + 1) * BLOCK:(2 * g + 2) * BLOCK])
             for g in range(4)], axis=1))
    att = jnp.concatenate(att_rows, axis=0) if nqb > 1 else att_rows[0]
    yd = att * sag_ref[0]

    zp = jnp.where(t > 0, zp_ref[0], 0.0)
    zn = jnp.where(t < nt - 1, zn_ref[0], 0.0)
    zext = jnp.concatenate([zp, zm_ref[0], zn], axis=0)
    conv = (zext[C_HALO - 1:C_HALO - 1 + tq] * cw_ref[0:1, :] + zext[C_HALO:C_HALO + tq] * cw_ref[1:2, :]
            + zext[C_HALO + 1:C_HALO + 1 + tq] * cw_ref[2:3, :])
    yc = cbg_ref[0] * conv
    mix = _dot(jnp.concatenate([yc.astype(BF16), yd.astype(BF16)], axis=1), wout_ref[...])
    out = x_ref[0] + gt_ref[0] * mix
    if final:
        ms = jnp.mean(out * out, axis=-1, keepdims=True)
        out = out * lax.rsqrt(ms + EPS) * fn_ref[...]
    o_ref[0] = out


def _odd_out(x, mod, q, k, v, kc, vc, sink, z, cbg, sag, conv_w, w_out, final_norm, *, tq, window):
    b, l, _ = x.shape
    nb = mod.shape[0]
    nt = l // tq
    nqb = tq // BLOCK
    lc = kc.shape[1]
    _, _, gt = _mod_specs(nb)
    tile = lambda w: pl.BlockSpec((1, tq, w), lambda bi, t: (bi, t, 0))
    prev_blk = pl.BlockSpec((1, BLOCK, KV_WIDTH), lambda bi, t: (bi, jnp.maximum(t * nqb - 1, 0), 0))
    next_blk = pl.BlockSpec((1, BLOCK, KV_WIDTH), lambda bi, t: (bi, jnp.minimum((t + 1) * nqb, l // BLOCK - 1), 0))
    zh = tq // C_HALO
    args, specs = [q], [tile(HALF)]
    if window:
        args += [k, k, k, v, v, v]
        specs += [prev_blk, tile(KV_WIDTH), next_blk, prev_blk, tile(KV_WIDTH), next_blk]
    args += [kc, vc, sink, z, z, z, cbg, sag, x, mod, conv_w, w_out]
    specs += [pl.BlockSpec((1, lc, KV_WIDTH), lambda bi, t: (bi, 0, 0)),
              pl.BlockSpec((1, lc, KV_WIDTH), lambda bi, t: (bi, 0, 0)),
              pl.BlockSpec(memory_space=pltpu.SMEM),
              pl.BlockSpec((1, C_HALO, HALF), lambda bi, t: (bi, jnp.maximum(t * zh - 1, 0), 0)),
              tile(HALF),
              pl.BlockSpec((1, C_HALO, HALF), lambda bi, t: (bi, jnp.minimum((t + 1) * zh, l // C_HALO - 1), 0)),
              tile(HALF), tile(HALF), tile(D_MODEL), gt,
              pl.BlockSpec((3, HALF), lambda bi, t: (0, 0)),
              pl.BlockSpec((D_MODEL, D_MODEL), lambda bi, t: (0, 0))]
    if final_norm is not None:
        args.append(final_norm)
        specs.append(pl.BlockSpec((1, D_MODEL), lambda bi, t: (0, 0)))
    return pl.pallas_call(
        functools.partial(_odd_out_kernel, tq=tq, nt=nt, window=window, final=final_norm is not None),
        grid=(b, nt),
        in_specs=specs,
        out_specs=tile(D_MODEL),
        out_shape=jax.ShapeDtypeStruct((b, l, D_MODEL), F32),
        compiler_params=_params("arbitrary", "arbitrary"),
        name="odd_out_window" if window else "odd_out_context",
    )(*args)


def _pair_heads_cols(w):
    lead = w.shape[:-1]
    return w.reshape(*lead, 2, 4, HEAD_DIM).swapaxes(-3, -2).reshape(*lead, HALF)


def _rope_tables(l, grid_w):
    nf = HEAD_DIM // 4
    pos = jnp.arange(l)
    inv = jnp.power(ROPE_BASE, -jnp.arange(nf, dtype=F32) / nf)
    ang = jnp.stack([(pos // grid_w).astype(F32)[:, None] * inv, (pos % grid_w).astype(F32)[:, None] * inv], axis=1)
    cos, sin = jnp.cos(ang), jnp.sin(ang)
    cos_t = jnp.stack([cos, cos], axis=2).reshape(l, HEAD_DIM)
    sin_t = jnp.stack([-sin, sin], axis=2).reshape(l, HEAD_DIM)
    return jnp.tile(cos_t, (1, 2)), jnp.tile(sin_t, (1, 2))


X_TILE = 512
GRID_W = 64


def kernel(x, c, ctx, c_ctx, e_norm, e_w_mod, e_b_mod, e_w_in, e_a_conv_w, e_a_conv_b, e_a_ln_g, e_a_ln_b,
           e_a_pw_w, e_a_pw_b, e_b_fw, e_w_out, o_norm, o_w_mod, o_b_mod, o_w_in, o_c_conv_w, o_sink, o_w_out,
           final_norm):
    b, l, _ = x.shape
    lc = ctx.shape[1]
    assert x.shape == (8, 4096, D_MODEL) and ctx.shape == (8, 256, D_MODEL)
    depth = e_norm.shape[0] + o_norm.shape[0]
    assert depth % 2 == 0

    cv = jnp.zeros((16, D_MODEL), F32).at[:b].set(c).at[b].set(c_ctx)
    mod_e = _modulation(cv, e_w_mod, e_b_mod)
    mod_o = _modulation(cv, o_w_mod, o_b_mod)
    gmat = _fourier_channel_mats(e_b_fw)

    cos_x, sin_x = _rope_tables(l, GRID_W)
    cos_c, sin_c = jnp.ones((lc, 128), F32), jnp.zeros((lc, 128), F32)

    row = lambda a: a.reshape(1, -1)
    xc = ctx
    for layer in range(depth):
        i = layer // 2
        ctx_needed = layer < depth - 1
        if layer % 2 == 0:
            mx, mc = mod_e[i, :b].reshape(b, 1, -1), mod_e[i, b:b + 1].reshape(1, 1, -1)
            rep = lambda a: jnp.broadcast_to(a[..., None, :], a.shape[:-1] + (SUBLANES, HALF))
            wts = (row(e_norm[i]), e_w_in[i].astype(BF16), rep(e_a_conv_w[i]), rep(e_a_conv_b[i]), row(e_a_ln_g[i]),
                   row(e_a_ln_b[i]), e_a_pw_w[i].astype(BF16), row(e_a_pw_b[i]), gmat[i], e_w_out[i].astype(BF16))
            x = _even_layer(x, mx, *wts, tl=X_TILE)
            if ctx_needed:
                xc = _even_layer(xc, mc, *wts, tl=lc)
        else:
            mx, mc = mod_o[i, :b].reshape(b, 1, -1), mod_o[i, b:b + 1].reshape(1, 1, -1)
            w = o_w_in[i]
            w_in = jnp.concatenate([w[:, :4 * HALF], _pair_heads_cols(w[:, 4 * HALF:5 * HALF]),
                                    w[:, 5 * HALF:5 * HALF + 2 * KV_WIDTH],
                                    _pair_heads_cols(w[:, 5 * HALF + 2 * KV_WIDTH:])], axis=1).astype(BF16)
            wo = o_w_out[i]
            wo_att = wo[HALF:].reshape(2, 4, HEAD_DIM, D_MODEL).swapaxes(0, 1).reshape(HALF, D_MODEL)
            w_out = jnp.concatenate([wo[:HALF], wo_att], axis=0).astype(BF16)
            sink = o_sink[i].reshape(2, 4).T.reshape(N_Q_HEADS)
            norm_g = row(o_norm[i])
            cbg_c, z_c, q_c, k_c, v_c, sag_c = _odd_in(xc, mc, norm_g, w_in, cos_c, sin_c, tl=lc)
            cbg, z, q, k, v, sag = _odd_in(x, mx, norm_g, w_in, cos_x, sin_x, tl=X_TILE)
            last = layer == depth - 1
            x = _odd_out(x, mx, q, k, v, k_c, v_c, sink, z, cbg, sag, o_c_conv_w[i], w_out,
                         row(final_norm) if last else None, tq=X_TILE, window=True)
            if ctx_needed:
                xc = _odd_out(xc, mc, q_c, None, None, k_c, v_c, sink, z_c, cbg_c, sag_c, o_c_conv_w[i], w_out,
                              None, tq=lc, window=False)
    return x
```

```python
import functools

import numpy as np
import jax
import jax.numpy as jnp
from jax import lax
from jax.experimental import pallas as pl
from jax.experimental.pallas import tpu as pltpu

D_MODEL = 1024
HALF = 512
A_CONV_W = 31
A_HALO = 16
C_HALO = 8
SUBLANES = 8
B_GROUPS = 4
B_GROUP_DIM = 128
HEAD_DIM = 64
N_Q_HEADS = 8
KV_WIDTH = 128
BLOCK = 128
ODD_IN = 3328
EVEN_IN = 2560
ROPE_BASE = 10000.0
NEG_INF = -1e30
LOG2E = 1.4426950408889634
Q_SCALE = HEAD_DIM ** -0.5 * LOG2E
EPS = 1e-6
RADIX = 16
LANES = 128
MXU_COLS = 256
FFT_COLS = MXU_COLS
PM_SPLIT = ((), (0, 1, 2), (3, 4, 5), ())
PM_ORDER = tuple(i for grp in PM_SPLIT for i in grp)
F32 = jnp.float32
BF16 = jnp.bfloat16
VMEM_LIMIT_BYTES = 48 * 1024 * 1024


def _params(*sem):
    return pltpu.CompilerParams(dimension_semantics=sem, vmem_limit_bytes=VMEM_LIMIT_BYTES)


def _dot(a, b):
    return jnp.dot(a, b, preferred_element_type=F32)


def _dot_nt(a, b):
    return lax.dot_general(a, b, (((1,), (1,)), ((), ())), preferred_element_type=F32)


def _pre_norm(x, g_ref, sh_ref, sc_ref):
    mult = g_ref[...] * (1.0 + sc_ref[0])
    ms = jnp.mean(x * x, axis=-1, keepdims=True)
    return (x * lax.rsqrt(ms + EPS)) * mult + sh_ref[0]


def _mod_kernel(c_ref, w_ref, b_ref, o_ref):
    o_ref[0] = _dot(jax.nn.silu(c_ref[...]), w_ref[0]) + b_ref[0]


def _modulation(cv, w_mod, b_mod):
    n = w_mod.shape[0]
    return pl.pallas_call(
        _mod_kernel,
        grid=(n, 3),
        in_specs=[pl.BlockSpec((16, D_MODEL), lambda l, j: (0, 0)),
                  pl.BlockSpec((1, D_MODEL, D_MODEL), lambda l, j: (l, 0, j)),
                  pl.BlockSpec((1, 1, D_MODEL), lambda l, j: (l, 0, j))],
        out_specs=pl.BlockSpec((1, 16, D_MODEL), lambda l, j: (l, 0, j)),
        out_shape=jax.ShapeDtypeStruct((n, 16, 3 * D_MODEL), F32),
        compiler_params=_params("arbitrary", "arbitrary"),
        name="modulation",
    )(cv, w_mod, b_mod.reshape(n, 1, 3 * D_MODEL))


def _mod_specs(nb):
    def spec(j):
        if nb == 1:
            return pl.BlockSpec((1, 1, D_MODEL), lambda b, t: (0, 0, j))
        return pl.BlockSpec((1, 1, D_MODEL), lambda b, t: (b, 0, j))
    return spec(0), spec(1), spec(2)


def _gprep_kernel(cd_ref, sd_ref, fw_ref, o_ref):
    o_ref[...] = jnp.zeros(o_ref.shape, o_ref.dtype)
    for g in range(B_GROUPS):
        fw = fw_ref[0, g]
        gc = jnp.dot(cd_ref[...], fw, preferred_element_type=F32, precision=lax.Precision.HIGHEST)
        gs = jnp.dot(sd_ref[...], fw, preferred_element_type=F32, precision=lax.Precision.HIGHEST)
        lo, hi = g * B_GROUP_DIM, (g + 1) * B_GROUP_DIM
        o_ref[0, lo:hi, lo:hi] = gc.astype(BF16)
        o_ref[0, lo:hi, HALF + lo:HALF + hi] = (-gs).astype(BF16)


def _fourier_channel_mats(fw):
    n = fw.shape[0]
    d = np.arange(B_GROUP_DIM)
    ang = 2.0 * np.pi * ((d[:, None] * d[None, :]) % B_GROUP_DIM) / B_GROUP_DIM
    cd = jnp.asarray(np.cos(ang) / np.sqrt(B_GROUP_DIM), F32)
    sd = jnp.asarray(np.sin(ang) / np.sqrt(B_GROUP_DIM), F32)
    mat = pl.BlockSpec((B_GROUP_DIM, B_GROUP_DIM), lambda i: (0, 0))
    return pl.pallas_call(
        _gprep_kernel,
        grid=(n,),
        in_specs=[mat, mat,
                  pl.BlockSpec((1, B_GROUPS, B_GROUP_DIM, B_GROUP_DIM), lambda i: (i, 0, 0, 0))],
        out_specs=pl.BlockSpec((1, HALF, 2 * HALF), lambda i: (i, 0, 0)),
        out_shape=jax.ShapeDtypeStruct((n, HALF, 2 * HALF), BF16),
        compiler_params=_params("arbitrary"),
        name="fourier_channel_mats",
    )(cd, sd, fw)


def _real_stack(m, real_only=False):
    top = np.concatenate([m.real, -m.imag], axis=1)
    if real_only:
        return top
    return np.concatenate([top, np.concatenate([m.imag, m.real], axis=1)], axis=0)


def _fft_tables_4096():
    r = np.arange(RADIX)
    eye = np.eye(RADIX)
    w16 = np.exp(-2j * np.pi * np.outer(r, r) / 16)
    w256 = np.exp(-2j * np.pi * np.outer(r, r) / 256)
    w4096 = np.exp(-2j * np.pi * np.outer(r, r) / 4096)
    ma = np.einsum("ka,kc,cd->kcad", w16, w4096, eye).reshape(256, 256) / 64.0
    mb = np.einsum("kb,kc,cd->kcbd", w16, w256, eye).reshape(256, 256)
    mc = np.einsum("kc,ae->kaec", w16, eye).reshape(256, 256)
    ph = 2.0 * np.pi * np.outer(r, r) / 256
    as_bf16 = lambda m: jnp.asarray(m, F32).astype(BF16)
    return (as_bf16(_real_stack(ma)), as_bf16(_real_stack(mb)), as_bf16(_real_stack(mc, real_only=True)),
            jnp.asarray(np.cos(ph), F32), jnp.asarray(np.sin(ph), F32))


def _fft_4096_kernel(z_ref, ma_ref, mb_ref, mc_ref, tc_ref, ts_ref, o_ref, s_scr):
    ct = z_ref.shape[-1]
    blk = RADIX * RADIX
    for b in range(RADIX):
        a1 = _dot(ma_ref[...], z_ref[0, :, :, b].reshape(2 * blk, ct))
        ar, ai = a1[:blk], a1[blk:]
        out_r, out_i = [], []
        for ka in range(RADIX):
            rr, ii = ar[ka * RADIX:(ka + 1) * RADIX], ai[ka * RADIX:(ka + 1) * RADIX]
            if b * ka == 0:
                out_r.append(rr)
                out_i.append(ii)
            else:
                c, s = tc_ref[b, ka], ts_ref[b, ka]
                out_r.append(rr * c + ii * s)
                out_i.append(ii * c - rr * s)
        s_scr[0, :, b] = jnp.concatenate(out_r, axis=0).astype(BF16).reshape(RADIX, RADIX, ct)
        s_scr[1, :, b] = jnp.concatenate(out_i, axis=0).astype(BF16).reshape(RADIX, RADIX, ct)
    for ka in range(RADIX):
        a2 = _dot(mb_ref[...], s_scr[:, ka].reshape(2 * blk, ct))
        s_scr[:, ka] = a2.astype(BF16).reshape(2, RADIX, RADIX, ct)
    for kb in range(RADIX):
        y = _dot(mc_ref[...], s_scr[:, :, kb].reshape(2 * blk, ct))
        o_ref[0, :, kb] = y.reshape(RADIX, RADIX, ct)


def _fft_4096(z):
    b = z.shape[0]
    ma, mb, mc, tc, ts = _fft_tables_4096()
    blk = RADIX * RADIX
    mat = lambda rows: pl.BlockSpec((rows, 2 * blk), lambda bi, j: (0, 0))
    smem = pl.BlockSpec(memory_space=pltpu.SMEM)
    y = pl.pallas_call(
        _fft_4096_kernel,
        grid=(b, HALF // FFT_COLS),
        in_specs=[pl.BlockSpec((1, 2, RADIX, RADIX, RADIX, FFT_COLS), lambda bi, j: (bi, 0, 0, 0, 0, j)),
                  mat(2 * blk), mat(2 * blk), mat(blk), smem, smem],
        out_specs=pl.BlockSpec((1, RADIX, RADIX, RADIX, FFT_COLS), lambda bi, j: (bi, 0, 0, 0, j)),
        out_shape=jax.ShapeDtypeStruct((b, RADIX, RADIX, RADIX, HALF), F32),
        scratch_shapes=[pltpu.VMEM((2, RADIX, RADIX, RADIX, FFT_COLS), BF16)],
        compiler_params=_params("arbitrary", "arbitrary"),
        name="fft_4096",
    )(z.reshape(b, 2, RADIX, RADIX, RADIX, HALF), ma, mb, mc, tc, ts)
    return y.reshape(b, RADIX ** 3, HALF)


def _fft_dense_kernel(z_ref, f_ref, o_ref):
    o_ref[0] = _dot(f_ref[...], z_ref[0].reshape(f_ref.shape[1], z_ref.shape[-1]))


def _fft_dense(z):
    b, _, l, _ = z.shape
    k = np.arange(l)
    ang = 2.0 * np.pi * ((k[:, None] * k[None, :]) % l) / l
    f = jnp.asarray(np.concatenate([np.cos(ang), np.sin(ang)], axis=1) / np.sqrt(l), F32).astype(BF16)
    return pl.pallas_call(
        _fft_dense_kernel,
        grid=(b, HALF // FFT_COLS),
        in_specs=[pl.BlockSpec((1, 2, l, FFT_COLS), lambda bi, j: (bi, 0, 0, j)),
                  pl.BlockSpec((l, 2 * l), lambda bi, j: (0, 0))],
        out_specs=pl.BlockSpec((1, l, FFT_COLS), lambda bi, j: (bi, 0, j)),
        out_shape=jax.ShapeDtypeStruct((b, l, HALF), F32),
        compiler_params=_params("arbitrary", "arbitrary"),
        name="fft_dense",
    )(z, f)


def _zeros_once_computed(tiles, zero, rows):
    acc = None
    for tile in tiles:
        bits = lax.bitcast_convert_type(tile, jnp.int32)
        bits = bits.reshape(tile.shape[0] // SUBLANES, SUBLANES, tile.shape[1])
        for i in range(bits.shape[0]):
            for c in range(0, tile.shape[1], LANES):
                part = bits[i, :, c:c + LANES]
                acc = part if acc is None else acc | part
    z = lax.bitcast_convert_type(acc & zero, F32)
    return jnp.broadcast_to(z[None], (rows // SUBLANES, SUBLANES, LANES)).reshape(rows, LANES)


def _even_in_kernel(zero_ref, xp_ref, xm_ref, xn_ref, sh_ref, sc_ref, g_ref, win_ref, cw_ref, cb_ref, lg_ref, lb_ref,
                    pw_ref, pb_ref, gm_ref, ya_ref, z_ref, sgb_ref, u_scr, c_scr, *, tl, nt):
    t = pl.program_id(1)
    ext = tl + 2 * A_HALO
    xe = jnp.concatenate([xp_ref[0], xm_ref[0], xn_ref[0]], axis=0)
    he = _pre_norm(xe, g_ref, sh_ref, sc_ref).astype(BF16)
    row = lax.broadcasted_iota(jnp.int32, (ext, 1), 0)
    inside = ((row >= A_HALO) | (t > 0)) & ((row < tl + A_HALO) | (t < nt - 1))
    sub = lax.broadcasted_iota(jnp.int32, (ext // SUBLANES, SUBLANES, LANES), 1)
    rc = 64
    off = A_HALO - (A_CONV_W - 1) // 2
    hm = he[A_HALO:A_HALO + tl]
    pm_tiles = []
    pad = jnp.zeros((ext, LANES), F32)
    for blk in range(HALF // LANES):
        lanes = slice(blk * LANES, (blk + 1) * LANES)
        pa = _dot(he, win_ref[:, 2 * blk * LANES:2 * (blk + 1) * LANES])
        u = jnp.where(inside, pa[:, :LANES] * jax.nn.sigmoid(pa[:, LANES:]), pad)
        if PM_SPLIT[blk]:
            new = [_dot(hm, win_ref[:, 2 * HALF + i * MXU_COLS:2 * HALF + (i + 1) * MXU_COLS])
                   for i in PM_SPLIT[blk]]
            pm_tiles += new
            pad = _zeros_once_computed(new, zero_ref[0], ext)
        u_scr[0, :, lanes] = u
        u3 = u.reshape(ext // SUBLANES, SUBLANES, LANES)
        for r in range(1, SUBLANES):
            rot = pltpu.roll(u3, SUBLANES - r, 1)
            nxt = jnp.concatenate([rot[1:], rot[:1]], axis=0)
            u_scr[r, :, lanes] = jnp.where(sub < SUBLANES - r, rot, nxt).reshape(ext, LANES)
        for r0 in range(0, tl, rc):
            acc = jnp.broadcast_to(cb_ref[:, lanes][None], (rc // SUBLANES, SUBLANES, LANES))
            for j in range(A_CONV_W):
                q, r = divmod(off + j, SUBLANES)
                uj = u_scr[r, r0 + SUBLANES * q:r0 + SUBLANES * q + rc, lanes]
                acc = acc + uj.reshape(rc // SUBLANES, SUBLANES, LANES) * cw_ref[j, :, lanes][None]
            c_scr[r0:r0 + rc, lanes] = acc.reshape(rc, LANES)
    pm = jnp.concatenate([pm_tiles[PM_ORDER.index(i)] for i in range(len(PM_ORDER))], axis=1)
    cv = c_scr[...]
    mu = jnp.mean(cv, axis=-1, keepdims=True)
    cen = cv - mu
    var = jnp.mean(cen * cen, axis=-1, keepdims=True)
    ln = cen * lax.rsqrt(var + EPS) * lg_ref[...] + lb_ref[...]
    ya = (_dot(jax.nn.silu(ln).astype(BF16), pw_ref[...]) + pb_ref[...]) * jax.nn.silu(pm[:, :HALF])
    ya_ref[0] = ya.astype(BF16)
    pq = _dot(pm[:, HALF:2 * HALF].astype(BF16), gm_ref[...])
    z_ref[0, 0] = pq[:, :HALF].astype(BF16)
    z_ref[0, 1] = pq[:, HALF:].astype(BF16)
    sgb_ref[0] = jax.nn.silu(pm[:, 2 * HALF:])


def _even_front(x, mod, norm_g, w_in, conv_w, conv_b, ln_g, ln_b, pw_w, pw_b, gmat, *, tl):
    b, l, _ = x.shape
    nb = mod.shape[0]
    nt = l // tl
    hb = tl // A_HALO
    sh, sc, _ = _mod_specs(nb)
    const = lambda shape: pl.BlockSpec(shape, lambda bi, t: (0,) * len(shape))
    tile = lambda w: pl.BlockSpec((1, tl, w), lambda bi, t: (bi, t, 0))
    ya, z, sgb = pl.pallas_call(
        functools.partial(_even_in_kernel, tl=tl, nt=nt),
        grid=(b, nt),
        in_specs=[pl.BlockSpec(memory_space=pltpu.SMEM),
                  pl.BlockSpec((1, A_HALO, D_MODEL), lambda bi, t: (bi, jnp.maximum(t * hb - 1, 0), 0)),
                  tile(D_MODEL),
                  pl.BlockSpec((1, A_HALO, D_MODEL), lambda bi, t: (bi, jnp.minimum((t + 1) * hb, l // A_HALO - 1), 0)),
                  sh, sc, const((1, D_MODEL)), const((D_MODEL, EVEN_IN)), const((A_CONV_W, SUBLANES, HALF)),
                  const((SUBLANES, HALF)), const((1, HALF)), const((1, HALF)), const((HALF, HALF)), const((1, HALF)),
                  const((HALF, 2 * HALF))],
        out_specs=[tile(HALF),
                   pl.BlockSpec((1, 2, tl, HALF), lambda bi, t: (bi, 0, t, 0)),
                   tile(HALF)],
        out_shape=[jax.ShapeDtypeStruct((b, l, HALF), BF16),
                   jax.ShapeDtypeStruct((b, 2, l, HALF), BF16),
                   jax.ShapeDtypeStruct((b, l, HALF), F32)],
        scratch_shapes=[pltpu.VMEM((SUBLANES, tl + 2 * A_HALO, HALF), F32), pltpu.VMEM((tl, HALF), F32)],
        compiler_params=_params("arbitrary", "arbitrary"),
        name="even_in",
    )(jnp.zeros((1,), jnp.int32), x, x, x, mod, mod, norm_g, w_in, conv_w, conv_b, ln_g, ln_b, pw_w, pw_b, gmat)

    y = _fft_4096(z) if l == RADIX ** 3 else _fft_dense(z)
    return ya, y, sgb


def _rope(t, cos_ref, sin_ref):
    reps = t.shape[1] // 128
    c = jnp.concatenate([cos_ref[...]] * reps, axis=1) if reps > 1 else cos_ref[...]
    s = jnp.concatenate([sin_ref[...]] * reps, axis=1) if reps > 1 else sin_ref[...]
    lane = lax.broadcasted_iota(jnp.int32, t.shape, 1)
    partner = jnp.where((lane & 16) == 0, pltpu.roll(t, t.shape[1] - 16, 1), pltpu.roll(t, 16, 1))
    return t * c + partner * s


def _even_out_odd_in_kernel(y_ref, sgb_ref, ya_ref, x_ref, gt_ref, wout_ref, sh_ref, sc_ref, g_ref, win_ref,
                            cos_ref, sin_ref, xo_ref, cbg_ref, z_ref, q_ref, k_ref, v_ref, sag_ref):
    yb = (y_ref[0] * sgb_ref[0]).astype(BF16)
    mix = _dot(jnp.concatenate([ya_ref[0], yb], axis=1), wout_ref[...])
    x = x_ref[0] + gt_ref[0] * mix
    xo_ref[0] = x
    h = _pre_norm(x, g_ref, sh_ref, sc_ref).astype(BF16)
    p = _dot(h, win_ref[...])
    cbg_ref[0] = p[:, 0:HALF] * jax.nn.silu(p[:, 3 * HALF:4 * HALF])
    z_ref[0] = p[:, HALF:2 * HALF] * p[:, 2 * HALF:3 * HALF]
    q0 = 4 * HALF
    q_ref[0] = (_rope(p[:, q0:q0 + HALF], cos_ref, sin_ref) * Q_SCALE).astype(BF16)
    k_ref[0] = _rope(p[:, q0 + HALF:q0 + HALF + KV_WIDTH], cos_ref, sin_ref).astype(BF16)
    v_ref[0] = p[:, q0 + HALF + KV_WIDTH:q0 + HALF + 2 * KV_WIDTH].astype(BF16)
    sag_ref[0] = jax.nn.silu(p[:, q0 + HALF + 2 * KV_WIDTH:])


def _even_out_odd_in(y, sgb, ya, x, mod_e, w_out, mod_o, norm_g, w_in, cos_t, sin_t, *, tl):
    b, l, _ = x.shape
    nb = mod_e.shape[0]
    _, _, gt = _mod_specs(nb)
    sh, sc, _ = _mod_specs(nb)
    tile = lambda w: pl.BlockSpec((1, tl, w), lambda bi, t: (bi, t, 0))
    const = lambda shape: pl.BlockSpec(shape, lambda bi, t: (0,) * len(shape))
    return pl.pallas_call(
        _even_out_odd_in_kernel,
        grid=(b, l // tl),
        in_specs=[tile(HALF), tile(HALF), tile(HALF), tile(D_MODEL), gt, const((D_MODEL, D_MODEL)),
                  sh, sc, const((1, D_MODEL)), const((D_MODEL, ODD_IN)),
                  pl.BlockSpec((tl, 128), lambda bi, t: (t, 0)),
                  pl.BlockSpec((tl, 128), lambda bi, t: (t, 0))],
        out_specs=[tile(D_MODEL), tile(HALF), tile(HALF), tile(HALF), tile(KV_WIDTH), tile(KV_WIDTH), tile(HALF)],
        out_shape=[jax.ShapeDtypeStruct((b, l, D_MODEL), F32),
                   jax.ShapeDtypeStruct((b, l, HALF), F32), jax.ShapeDtypeStruct((b, l, HALF), F32),
                   jax.ShapeDtypeStruct((b, l, HALF), BF16), jax.ShapeDtypeStruct((b, l, KV_WIDTH), BF16),
                   jax.ShapeDtypeStruct((b, l, KV_WIDTH), BF16), jax.ShapeDtypeStruct((b, l, HALF), F32)],
        compiler_params=_params("arbitrary", "arbitrary"),
        name="even_out_odd_in",
    )(y, sgb, ya, x, mod_e, w_out, mod_o, mod_o, norm_g, w_in, cos_t, sin_t)


def _odd_out_kernel(*refs, tq, nt, window, final):
    refs = list(refs)
    q_ref = refs.pop(0)
    if window:
        kp_ref, km_ref, kn_ref, vp_ref, vm_ref, vn_ref = refs[:6]
        refs = refs[6:]
    kc_ref, vc_ref, sink_ref, zp_ref, zm_ref, zn_ref, cbg_ref, sag_ref, x_ref, gt_ref, cw_ref, wout_ref = refs[:12]
    refs = refs[12:]
    fn_ref = refs.pop(0) if final else None
    o_ref = refs.pop(0)

    t = pl.program_id(1)
    nqb = tq // BLOCK
    lane = lax.broadcasted_iota(jnp.int32, (BLOCK, 128), 1)
    low = lane < HEAD_DIM
    kc, vc = kc_ref[0], vc_ref[0]
    if window:
        kext = jnp.concatenate([kp_ref[0], km_ref[0], kn_ref[0]], axis=0)
        vext = jnp.concatenate([vp_ref[0], vm_ref[0], vn_ref[0]], axis=0)
        qi = lax.broadcasted_iota(jnp.int32, (BLOCK, 3 * BLOCK), 0)
        kj = lax.broadcasted_iota(jnp.int32, (BLOCK, 3 * BLOCK), 1)
        band = (kj - qi >= 0) & (kj - qi <= 2 * BLOCK)

    att_rows = []
    for qb in range(nqb):
        qg = q_ref[0, qb * BLOCK:(qb + 1) * BLOCK, :]
        zero = jnp.zeros((BLOCK, 128), BF16)
        slots = []
        for g in range(4):
            qq = qg[:, g * 128:(g + 1) * 128]
            slots += [jnp.where(low, qq, zero), jnp.where(low, zero, qq)]
        qs = jnp.concatenate(slots, axis=0)
        s_c = _dot_nt(qs, kc)
        if window:
            s_w = _dot_nt(qs, kext[qb * BLOCK:(qb + 3) * BLOCK])
            valid = band
            if qb == 0:
                valid = valid & ((kj >= BLOCK) | (t > 0))
            if qb == nqb - 1:
                valid = valid & ((kj < 2 * BLOCK) | (t < nt - 1))
        pws, pcs, dens = [], [], []
        for s in range(8):
            sink = sink_ref[s] * LOG2E
            sc_s = s_c[s * BLOCK:(s + 1) * BLOCK]
            m = jnp.maximum(jnp.max(sc_s, axis=-1, keepdims=True), sink)
            if window:
                sw_s = jnp.where(valid, s_w[s * BLOCK:(s + 1) * BLOCK], NEG_INF)
                m = jnp.maximum(m, jnp.max(sw_s, axis=-1, keepdims=True))
            pc = jnp.exp2(sc_s - m)
            den = jnp.exp2(sink - m) + jnp.sum(pc, axis=-1, keepdims=True)
            if window:
                pw = jnp.exp2(sw_s - m)
                den = den + jnp.sum(pw, axis=-1, keepdims=True)
                pws.append(pw.astype(BF16))
            pcs.append(pc.astype(BF16))
            dens.append(den)
        o = _dot(jnp.concatenate(pcs, axis=0), vc)
        if window:
            o = o + _dot(jnp.concatenate(pws, axis=0), vext[qb * BLOCK:(qb + 3) * BLOCK])
        o = o / jnp.concatenate(dens, axis=0)
        att_rows.append(jnp.concatenate(
            [jnp.where(low, o[(2 * g) * BLOCK:(2 * g + 1) * BLOCK], o[(2 * g + 1) * BLOCK:(2 * g + 2) * BLOCK])
             for g in range(4)], axis=1))
    att = jnp.concatenate(att_rows, axis=0) if nqb > 1 else att_rows[0]
    yd = att * sag_ref[0]

    zp = jnp.where(t > 0, zp_ref[0], 0.0)
    zn = jnp.where(t < nt - 1, zn_ref[0], 0.0)
    zext = jnp.concatenate([zp, zm_ref[0], zn], axis=0)
    conv = (zext[C_HALO - 1:C_HALO - 1 + tq] * cw_ref[0:1, :] + zext[C_HALO:C_HALO + tq] * cw_ref[1:2, :]
            + zext[C_HALO + 1:C_HALO + 1 + tq] * cw_ref[2:3, :])
    yc = cbg_ref[0] * conv
    mix = _dot(jnp.concatenate([yc.astype(BF16), yd.astype(BF16)], axis=1), wout_ref[...])
    out = x_ref[0] + gt_ref[0] * mix
    if final:
        ms = jnp.mean(out * out, axis=-1, keepdims=True)
        out = out * lax.rsqrt(ms + EPS) * fn_ref[...]
    o_ref[0] = out


def _odd_out(x, mod, q, k, v, kc, vc, sink, z, cbg, sag, conv_w, w_out, final_norm, *, tq, window):
    b, l, _ = x.shape
    nb = mod.shape[0]
    nt = l // tq
    nqb = tq // BLOCK
    lc = kc.shape[1]
    _, _, gt = _mod_specs(nb)
    tile = lambda w: pl.BlockSpec((1, tq, w), lambda bi, t: (bi, t, 0))
    prev_blk = pl.BlockSpec((1, BLOCK, KV_WIDTH), lambda bi, t: (bi, jnp.maximum(t * nqb - 1, 0), 0))
    next_blk = pl.BlockSpec((1, BLOCK, KV_WIDTH), lambda bi, t: (bi, jnp.minimum((t + 1) * nqb, l // BLOCK - 1), 0))
    zh = tq // C_HALO
    args, specs = [q], [tile(HALF)]
    if window:
        args += [k, k, k, v, v, v]
        specs += [prev_blk, tile(KV_WIDTH), next_blk, prev_blk, tile(KV_WIDTH), next_blk]
    args += [kc, vc, sink, z, z, z, cbg, sag, x, mod, conv_w, w_out]
    specs += [pl.BlockSpec((1, lc, KV_WIDTH), lambda bi, t: (bi, 0, 0)),
              pl.BlockSpec((1, lc, KV_WIDTH), lambda bi, t: (bi, 0, 0)),
              pl.BlockSpec(memory_space=pltpu.SMEM),
              pl.BlockSpec((1, C_HALO, HALF), lambda bi, t: (bi, jnp.maximum(t * zh - 1, 0), 0)),
              tile(HALF),
              pl.BlockSpec((1, C_HALO, HALF), lambda bi, t: (bi, jnp.minimum((t + 1) * zh, l // C_HALO - 1), 0)),
              tile(HALF), tile(HALF), tile(D_MODEL), gt,
              pl.BlockSpec((3, HALF), lambda bi, t: (0, 0)),
              pl.BlockSpec((D_MODEL, D_MODEL), lambda bi, t: (0, 0))]
    if final_norm is not None:
        args.append(final_norm)
        specs.append(pl.BlockSpec((1, D_MODEL), lambda bi, t: (0, 0)))
    return pl.pallas_call(
        functools.partial(_odd_out_kernel, tq=tq, nt=nt, window=window, final=final_norm is not None),
        grid=(b, nt),
        in_specs=specs,
        out_specs=tile(D_MODEL),
        out_shape=jax.ShapeDtypeStruct((b, l, D_MODEL), F32),
        compiler_params=_params("arbitrary", "arbitrary"),
        name="odd_out_window" if window else "odd_out_context",
    )(*args)


def _pair_heads_cols(w):
    lead = w.shape[:-1]
    return w.reshape(*lead, 2, 4, HEAD_DIM).swapaxes(-3, -2).reshape(*lead, HALF)


def _rope_tables(l, grid_w):
    nf = HEAD_DIM // 4
    pos = jnp.arange(l)
    inv = jnp.power(ROPE_BASE, -jnp.arange(nf, dtype=F32) / nf)
    ang = jnp.stack([(pos // grid_w).astype(F32)[:, None] * inv, (pos % grid_w).astype(F32)[:, None] * inv], axis=1)
    cos, sin = jnp.cos(ang), jnp.sin(ang)
    cos_t = jnp.stack([cos, cos], axis=2).reshape(l, HEAD_DIM)
    sin_t = jnp.stack([-sin, sin], axis=2).reshape(l, HEAD_DIM)
    return jnp.tile(cos_t, (1, 2)), jnp.tile(sin_t, (1, 2))


X_TILE = 512
GRID_W = 64


def kernel(x, c, ctx, c_ctx, e_norm, e_w_mod, e_b_mod, e_w_in, e_a_conv_w, e_a_conv_b, e_a_ln_g, e_a_ln_b,
           e_a_pw_w, e_a_pw_b, e_b_fw, e_w_out, o_norm, o_w_mod, o_b_mod, o_w_in, o_c_conv_w, o_sink, o_w_out,
           final_norm):
    b, l, _ = x.shape
    lc = ctx.shape[1]
    assert x.shape == (8, 4096, D_MODEL) and ctx.shape == (8, 256, D_MODEL)
    depth = e_norm.shape[0] + o_norm.shape[0]
    assert depth % 2 == 0

    cv = jnp.zeros((16, D_MODEL), F32).at[:b].set(c).at[b].set(c_ctx)
    mod_e = _modulation(cv, e_w_mod, e_b_mod)
    mod_o = _modulation(cv, o_w_mod, o_b_mod)
    gmat = _fourier_channel_mats(e_b_fw)

    cos_x, sin_x = _rope_tables(l, GRID_W)
    cos_c, sin_c = jnp.ones((lc, 128), F32), jnp.zeros((lc, 128), F32)

    row = lambda a: a.reshape(1, -1)
    rep = lambda a: jnp.broadcast_to(a[..., None, :], a.shape[:-1] + (SUBLANES, HALF))
    xc = ctx
    for i in range(depth // 2):
        last = i == depth // 2 - 1
        me_x, me_c = mod_e[i, :b].reshape(b, 1, -1), mod_e[i, b:b + 1].reshape(1, 1, -1)
        mo_x, mo_c = mod_o[i, :b].reshape(b, 1, -1), mod_o[i, b:b + 1].reshape(1, 1, -1)
        w = e_w_in[i]
        w_glu = w[:, :2 * HALF].reshape(D_MODEL, 2, HALF // LANES, LANES).swapaxes(1, 2).reshape(D_MODEL, 2 * HALF)
        we_in = jnp.concatenate([w_glu, w[:, 2 * HALF:]], axis=1).astype(BF16)
        front = (row(e_norm[i]), we_in, rep(e_a_conv_w[i]), rep(e_a_conv_b[i]), row(e_a_ln_g[i]),
                 row(e_a_ln_b[i]), e_a_pw_w[i].astype(BF16), row(e_a_pw_b[i]), gmat[i])
        we_out = e_w_out[i].astype(BF16)
        w = o_w_in[i]
        wo_in = jnp.concatenate([w[:, :4 * HALF], _pair_heads_cols(w[:, 4 * HALF:5 * HALF]),
                                 w[:, 5 * HALF:5 * HALF + 2 * KV_WIDTH],
                                 _pair_heads_cols(w[:, 5 * HALF + 2 * KV_WIDTH:])], axis=1).astype(BF16)
        wo = o_w_out[i]
        wo_att = wo[HALF:].reshape(2, 4, HEAD_DIM, D_MODEL).swapaxes(0, 1).reshape(HALF, D_MODEL)
        wo_out = jnp.concatenate([wo[:HALF], wo_att], axis=0).astype(BF16)
        sink = o_sink[i].reshape(2, 4).T.reshape(N_Q_HEADS)
        norm_o = row(o_norm[i])

        ya, y, sgb = _even_front(xc, me_c, *front, tl=lc)
        xc, cbg_c, z_c, q_c, k_c, v_c, sag_c = _even_out_odd_in(y, sgb, ya, xc, me_c, we_out, mo_c, norm_o, wo_in,
                                                                cos_c, sin_c, tl=lc)
        ya, y, sgb = _even_front(x, me_x, *front, tl=X_TILE)
        x, cbg, z, q, k, v, sag = _even_out_odd_in(y, sgb, ya, x, me_x, we_out, mo_x, norm_o, wo_in,
                                                   cos_x, sin_x, tl=X_TILE)
        x = _odd_out(x, mo_x, q, k, v, k_c, v_c, sink, z, cbg, sag, o_c_conv_w[i], wo_out,
                     row(final_norm) if last else None, tq=X_TILE, window=True)
        if not last:
            xc = _odd_out(xc, mo_c, q_c, None, None, k_c, v_c, sink, z_c, cbg_c, sag_c, o_c_conv_w[i], wo_out,
                          None, tq=lc, window=False)
    return x
```

```python
import functools

import numpy as np
import jax
import jax.numpy as jnp
from jax import lax
from jax.experimental import pallas as pl
from jax.experimental.pallas import tpu as pltpu

D_MODEL = 1024
HALF = 512
A_CONV_W = 31
A_HALO = 16
C_HALO = 8
SUBLANES = 8
B_GROUPS = 4
B_GROUP_DIM = 128
HEAD_DIM = 64
N_Q_HEADS = 8
KV_WIDTH = 128
BLOCK = 128
ODD_IN = 3328
EVEN_IN = 2560
ROPE_BASE = 10000.0
NEG_INF = -1e30
LOG2E = 1.4426950408889634
Q_SCALE = HEAD_DIM ** -0.5 * LOG2E
EPS = 1e-6
RADIX = 16
MOD_ROWS = 16
CTX_ROW = 8
LANES = 128
MXU_COLS = 256
MXU_ROWS = 16
FFT_COLS = MXU_COLS
PM_SPLIT = ((), (0,), (1, 2), (3,))
PM_TAIL = (4, 5)
PM_ORDER = tuple(i for grp in PM_SPLIT for i in grp) + PM_TAIL
F32 = jnp.float32
BF16 = jnp.bfloat16
VMEM_LIMIT_BYTES = 48 * 1024 * 1024


def _params(*sem):
    return pltpu.CompilerParams(dimension_semantics=sem, vmem_limit_bytes=VMEM_LIMIT_BYTES)


def _dot(a, b):
    return jnp.dot(a, b, preferred_element_type=F32)


def _dot_nt(a, b):
    return lax.dot_general(a, b, (((1,), (1,)), ((), ())), preferred_element_type=F32)


def _pre_norm(x, g_ref, sh_ref, sc_ref):
    mult = g_ref[...] * (1.0 + sc_ref[...])
    ms = jnp.mean(x * x, axis=-1, keepdims=True)
    return (x * lax.rsqrt(ms + EPS)) * mult + sh_ref[...]


def _layer_spec(layer, shape):
    return pl.BlockSpec((None,) + tuple(shape), lambda bi, t: (layer,) + (0,) * len(shape))


def _mod_kernel(c_ref, w_ref, b_ref, o_ref):
    o_ref[0, :, 0, :] = _dot(jax.nn.silu(c_ref[...]), w_ref[0]) + b_ref[0]


def _modulation(cv, w_mod, b_mod):
    n = w_mod.shape[0]
    return pl.pallas_call(
        _mod_kernel,
        grid=(n, 3),
        in_specs=[pl.BlockSpec((MOD_ROWS, D_MODEL), lambda l, j: (0, 0)),
                  pl.BlockSpec((1, D_MODEL, D_MODEL), lambda l, j: (l, 0, j)),
                  pl.BlockSpec((1, 1, D_MODEL), lambda l, j: (l, 0, j))],
        out_specs=pl.BlockSpec((1, MOD_ROWS, 1, D_MODEL), lambda l, j: (l, 0, 0, j)),
        out_shape=jax.ShapeDtypeStruct((n, MOD_ROWS, 1, 3 * D_MODEL), F32),
        compiler_params=_params("arbitrary", "arbitrary"),
        name="modulation",
    )(cv, w_mod, b_mod.reshape(n, 1, 3 * D_MODEL))


def _mod_specs(layer, row):
    def spec(j):
        if row is None:
            return pl.BlockSpec((None, None, 1, D_MODEL), lambda b, t: (layer, b, 0, j))
        return pl.BlockSpec((None, None, 1, D_MODEL), lambda b, t: (layer, row, 0, j))
    return spec(0), spec(1), spec(2)


def _gprep_kernel(cd_ref, sd_ref, fw_ref, o_ref):
    o_ref[...] = jnp.zeros(o_ref.shape, o_ref.dtype)
    for g in range(B_GROUPS):
        fw = fw_ref[0, g]
        gc = jnp.dot(cd_ref[...], fw, preferred_element_type=F32, precision=lax.Precision.HIGHEST)
        gs = jnp.dot(sd_ref[...], fw, preferred_element_type=F32, precision=lax.Precision.HIGHEST)
        lo, hi = g * B_GROUP_DIM, (g + 1) * B_GROUP_DIM
        o_ref[0, lo:hi, lo:hi] = gc.astype(BF16)
        o_ref[0, lo:hi, HALF + lo:HALF + hi] = (-gs).astype(BF16)


def _fourier_channel_mats(fw):
    n = fw.shape[0]
    d = np.arange(B_GROUP_DIM)
    ang = 2.0 * np.pi * ((d[:, None] * d[None, :]) % B_GROUP_DIM) / B_GROUP_DIM
    cd = jnp.asarray(np.cos(ang) / np.sqrt(B_GROUP_DIM), F32)
    sd = jnp.asarray(np.sin(ang) / np.sqrt(B_GROUP_DIM), F32)
    mat = pl.BlockSpec((B_GROUP_DIM, B_GROUP_DIM), lambda i: (0, 0))
    return pl.pallas_call(
        _gprep_kernel,
        grid=(n,),
        in_specs=[mat, mat,
                  pl.BlockSpec((1, B_GROUPS, B_GROUP_DIM, B_GROUP_DIM), lambda i: (i, 0, 0, 0))],
        out_specs=pl.BlockSpec((1, HALF, 2 * HALF), lambda i: (i, 0, 0)),
        out_shape=jax.ShapeDtypeStruct((n, HALF, 2 * HALF), BF16),
        compiler_params=_params("arbitrary"),
        name="fourier_channel_mats",
    )(cd, sd, fw)


def _real_stack(m, real_only=False):
    top = np.concatenate([m.real, -m.imag], axis=1)
    if real_only:
        return top
    return np.concatenate([top, np.concatenate([m.imag, m.real], axis=1)], axis=0)


def _fft_tables_4096():
    r = np.arange(RADIX)
    eye = np.eye(RADIX)
    w16 = np.exp(-2j * np.pi * np.outer(r, r) / 16)
    w256 = np.exp(-2j * np.pi * np.outer(r, r) / 256)
    w4096 = np.exp(-2j * np.pi * np.outer(r, r) / 4096)
    ma = np.einsum("ka,kc,cd->kcad", w16, w4096, eye).reshape(256, 256) / 64.0
    mb = np.einsum("kb,kc,cd->kcbd", w16, w256, eye).reshape(256, 256)
    mc = np.einsum("kc,ae->kaec", w16, eye).reshape(256, 256)
    ph = 2.0 * np.pi * np.outer(r, r) / 256
    as_bf16 = lambda m: jnp.asarray(m, F32).astype(BF16)
    return (as_bf16(_real_stack(ma)), as_bf16(_real_stack(mb)), as_bf16(_real_stack(mc, real_only=True)),
            jnp.asarray(np.cos(ph), F32), jnp.asarray(np.sin(ph), F32))


def _fft_4096_kernel(z_ref, ma_ref, mb_ref, mc_ref, tc_ref, ts_ref, o_ref, s_scr):
    ct = z_ref.shape[-1]
    blk = RADIX * RADIX
    for b in range(RADIX):
        a1 = _dot(ma_ref[...], z_ref[0, :, :, b].reshape(2 * blk, ct))
        ar, ai = a1[:blk], a1[blk:]
        out_r, out_i = [], []
        for ka in range(RADIX):
            rr, ii = ar[ka * RADIX:(ka + 1) * RADIX], ai[ka * RADIX:(ka + 1) * RADIX]
            if b * ka == 0:
                out_r.append(rr)
                out_i.append(ii)
            else:
                c, s = tc_ref[b, ka], ts_ref[b, ka]
                out_r.append(rr * c + ii * s)
                out_i.append(ii * c - rr * s)
        s_scr[0, :, b] = jnp.concatenate(out_r, axis=0).astype(BF16).reshape(RADIX, RADIX, ct)
        s_scr[1, :, b] = jnp.concatenate(out_i, axis=0).astype(BF16).reshape(RADIX, RADIX, ct)
    for ka in range(RADIX):
        a2 = _dot(mb_ref[...], s_scr[:, ka].reshape(2 * blk, ct))
        s_scr[:, ka] = a2.astype(BF16).reshape(2, RADIX, RADIX, ct)
    for kb in range(RADIX):
        y = _dot(mc_ref[...], s_scr[:, :, kb].reshape(2 * blk, ct))
        o_ref[0, :, kb] = y.reshape(RADIX, RADIX, ct)


def _fft_4096(z):
    b = z.shape[0]
    ma, mb, mc, tc, ts = _fft_tables_4096()
    blk = RADIX * RADIX
    mat = lambda rows: pl.BlockSpec((rows, 2 * blk), lambda bi, j: (0, 0))
    smem = pl.BlockSpec(memory_space=pltpu.SMEM)
    y = pl.pallas_call(
        _fft_4096_kernel,
        grid=(b, HALF // FFT_COLS),
        in_specs=[pl.BlockSpec((1, 2, RADIX, RADIX, RADIX, FFT_COLS), lambda bi, j: (bi, 0, 0, 0, 0, j)),
                  mat(2 * blk), mat(2 * blk), mat(blk), smem, smem],
        out_specs=pl.BlockSpec((1, RADIX, RADIX, RADIX, FFT_COLS), lambda bi, j: (bi, 0, 0, 0, j)),
        out_shape=jax.ShapeDtypeStruct((b, RADIX, RADIX, RADIX, HALF), F32),
        scratch_shapes=[pltpu.VMEM((2, RADIX, RADIX, RADIX, FFT_COLS), BF16)],
        compiler_params=_params("arbitrary", "arbitrary"),
        name="fft_4096",
    )(z.reshape(b, 2, RADIX, RADIX, RADIX, HALF), ma, mb, mc, tc, ts)
    return y.reshape(b, RADIX ** 3, HALF)


def _fft_dense_kernel(z_ref, f_ref, o_ref):
    o_ref[0] = _dot(f_ref[...], z_ref[0].reshape(f_ref.shape[1], z_ref.shape[-1]))


def _fft_dense(z):
    b, _, l, _ = z.shape
    k = np.arange(l)
    ang = 2.0 * np.pi * ((k[:, None] * k[None, :]) % l) / l
    f = jnp.asarray(np.concatenate([np.cos(ang), np.sin(ang)], axis=1) / np.sqrt(l), F32).astype(BF16)
    return pl.pallas_call(
        _fft_dense_kernel,
        grid=(b, HALF // FFT_COLS),
        in_specs=[pl.BlockSpec((1, 2, l, FFT_COLS), lambda bi, j: (bi, 0, 0, j)),
                  pl.BlockSpec((l, 2 * l), lambda bi, j: (0, 0))],
        out_specs=pl.BlockSpec((1, l, FFT_COLS), lambda bi, j: (bi, 0, j)),
        out_shape=jax.ShapeDtypeStruct((b, l, HALF), F32),
        compiler_params=_params("arbitrary", "arbitrary"),
        name="fft_dense",
    )(z, f)


def _zeros_once_computed(tiles, zero, rows):
    acc = None
    for tile in tiles:
        bits = lax.bitcast_convert_type(tile, jnp.int32)
        bits = bits.reshape(tile.shape[0] // SUBLANES, SUBLANES, tile.shape[1])
        for i in range(0, bits.shape[0], MXU_ROWS // SUBLANES):
            for c in range(0, tile.shape[1], MXU_COLS):
                part = bits[i, :, c:c + LANES]
                acc = part if acc is None else acc | part
    z = lax.bitcast_convert_type(acc & zero, F32)
    return jnp.broadcast_to(z[None], (rows // SUBLANES, SUBLANES, LANES)).reshape(rows, LANES)


def _even_in_kernel(zero_ref, xp_ref, xm_ref, xn_ref, sh_ref, sc_ref, g_ref, win_ref, cw_ref, cb_ref, lg_ref, lb_ref,
                    pw_ref, pb_ref, gm_ref, ya_ref, z_ref, sgb_ref, u_scr, c_scr, *, tl, nt):
    t = pl.program_id(1)
    ext = tl + 2 * A_HALO
    xe = jnp.concatenate([xp_ref[0], xm_ref[0], xn_ref[0]], axis=0)
    he = _pre_norm(xe, g_ref, sh_ref, sc_ref).astype(BF16)
    row = lax.broadcasted_iota(jnp.int32, (ext, 1), 0)
    inside = ((row >= A_HALO) | (t > 0)) & ((row < tl + A_HALO) | (t < nt - 1))
    sub = lax.broadcasted_iota(jnp.int32, (ext // SUBLANES, SUBLANES, LANES), 1)
    rc = 64
    off = A_HALO - (A_CONV_W - 1) // 2
    hm = he[A_HALO:A_HALO + tl]
    pm_tiles = []
    pad = jnp.zeros((ext, LANES), F32)
    for blk in range(HALF // LANES):
        lanes = slice(blk * LANES, (blk + 1) * LANES)
        pa = _dot(he, win_ref[:, 2 * blk * LANES:2 * (blk + 1) * LANES])
        u = jnp.where(inside, pa[:, :LANES] * jax.nn.sigmoid(pa[:, LANES:]), pad)
        if PM_SPLIT[blk]:
            new = [_dot(hm, win_ref[:, 2 * HALF + i * MXU_COLS:2 * HALF + (i + 1) * MXU_COLS])
                   for i in PM_SPLIT[blk]]
            pm_tiles += new
            pad = _zeros_once_computed(new, zero_ref[0], ext)
        u_scr[0, :, lanes] = u
        u3 = u.reshape(ext // SUBLANES, SUBLANES, LANES)
        for r in range(1, SUBLANES):
            rot = pltpu.roll(u3, SUBLANES - r, 1)
            nxt = jnp.concatenate([rot[1:], rot[:1]], axis=0)
            u_scr[r, :, lanes] = jnp.where(sub < SUBLANES - r, rot, nxt).reshape(ext, LANES)
        for r0 in range(0, tl, rc):
            acc = jnp.broadcast_to(cb_ref[:, lanes][None], (rc // SUBLANES, SUBLANES, LANES))
            for j in range(A_CONV_W):
                q, r = divmod(off + j, SUBLANES)
                uj = u_scr[r, r0 + SUBLANES * q:r0 + SUBLANES * q + rc, lanes]
                acc = acc + uj.reshape(rc // SUBLANES, SUBLANES, LANES) * cw_ref[j, :, lanes][None]
            c_scr[r0:r0 + rc, lanes] = acc.reshape(rc, LANES)
    pm_tiles += [_dot(hm, win_ref[:, 2 * HALF + i * MXU_COLS:2 * HALF + (i + 1) * MXU_COLS]) for i in PM_TAIL]
    pm = jnp.concatenate([pm_tiles[PM_ORDER.index(i)] for i in range(len(PM_ORDER))], axis=1)
    pq = _dot(pm[:, HALF:2 * HALF].astype(BF16), gm_ref[...])
    tie = _zeros_once_computed(pm_tiles[-len(PM_TAIL):] + [pq], zero_ref[0], tl)
    cv = c_scr[...] + jnp.concatenate([tie] * (HALF // LANES), axis=1)
    mu = jnp.mean(cv, axis=-1, keepdims=True)
    cen = cv - mu
    var = jnp.mean(cen * cen, axis=-1, keepdims=True)
    ln = cen * lax.rsqrt(var + EPS) * lg_ref[...] + lb_ref[...]
    ya = (_dot(jax.nn.silu(ln).astype(BF16), pw_ref[...]) + pb_ref[...]) * jax.nn.silu(pm[:, :HALF])
    ya_ref[0] = ya.astype(BF16)
    z_ref[0, 0] = pq[:, :HALF].astype(BF16)
    z_ref[0, 1] = pq[:, HALF:].astype(BF16)
    sgb_ref[0] = jax.nn.silu(pm[:, 2 * HALF:])


def _even_front(x, mod, norm_g, w_in, conv_w, conv_b, ln_g, ln_b, pw_w, pw_b, gmat, *, layer, mod_row, tl):
    b, l, _ = x.shape
    nt = l // tl
    hb = tl // A_HALO
    sh, sc, _ = _mod_specs(layer, mod_row)
    const = functools.partial(_layer_spec, layer)
    tile = lambda w: pl.BlockSpec((1, tl, w), lambda bi, t: (bi, t, 0))
    ya, z, sgb = pl.pallas_call(
        functools.partial(_even_in_kernel, tl=tl, nt=nt),
        grid=(b, nt),
        in_specs=[pl.BlockSpec(memory_space=pltpu.SMEM),
                  pl.BlockSpec((1, A_HALO, D_MODEL), lambda bi, t: (bi, jnp.maximum(t * hb - 1, 0), 0)),
                  tile(D_MODEL),
                  pl.BlockSpec((1, A_HALO, D_MODEL), lambda bi, t: (bi, jnp.minimum((t + 1) * hb, l // A_HALO - 1), 0)),
                  sh, sc, const((1, D_MODEL)), const((D_MODEL, EVEN_IN)), const((A_CONV_W, SUBLANES, HALF)),
                  const((SUBLANES, HALF)), const((1, HALF)), const((1, HALF)), const((HALF, HALF)), const((1, HALF)),
                  const((HALF, 2 * HALF))],
        out_specs=[tile(HALF),
                   pl.BlockSpec((1, 2, tl, HALF), lambda bi, t: (bi, 0, t, 0)),
                   tile(HALF)],
        out_shape=[jax.ShapeDtypeStruct((b, l, HALF), BF16),
                   jax.ShapeDtypeStruct((b, 2, l, HALF), BF16),
                   jax.ShapeDtypeStruct((b, l, HALF), F32)],
        scratch_shapes=[pltpu.VMEM((SUBLANES, tl + 2 * A_HALO, HALF), F32), pltpu.VMEM((tl, HALF), F32)],
        compiler_params=_params("arbitrary", "arbitrary"),
        name="even_in",
    )(jnp.zeros((1,), jnp.int32), x, x, x, mod, mod, norm_g, w_in, conv_w, conv_b, ln_g, ln_b, pw_w, pw_b, gmat)

    y = _fft_4096(z) if l == RADIX ** 3 else _fft_dense(z)
    return ya, y, sgb


def _rope(t, cos_ref, sin_ref):
    reps = t.shape[1] // 128
    c = jnp.concatenate([cos_ref[...]] * reps, axis=1) if reps > 1 else cos_ref[...]
    s = jnp.concatenate([sin_ref[...]] * reps, axis=1) if reps > 1 else sin_ref[...]
    lane = lax.broadcasted_iota(jnp.int32, t.shape, 1)
    partner = jnp.where((lane & 16) == 0, pltpu.roll(t, t.shape[1] - 16, 1), pltpu.roll(t, 16, 1))
    return t * c + partner * s


def _even_out_odd_in_kernel(y_ref, sgb_ref, ya_ref, x_ref, gt_ref, wout_ref, sh_ref, sc_ref, g_ref, win_ref,
                            cos_ref, sin_ref, xo_ref, cbg_ref, z_ref, q_ref, k_ref, v_ref, sag_ref):
    yb = (y_ref[0] * sgb_ref[0]).astype(BF16)
    mix = _dot(jnp.concatenate([ya_ref[0], yb], axis=1), wout_ref[...])
    x = x_ref[0] + gt_ref[...] * mix
    xo_ref[0] = x
    h = _pre_norm(x, g_ref, sh_ref, sc_ref).astype(BF16)
    p = _dot(h, win_ref[...])
    cbg_ref[0] = p[:, 0:HALF] * jax.nn.silu(p[:, 3 * HALF:4 * HALF])
    z_ref[0] = p[:, HALF:2 * HALF] * p[:, 2 * HALF:3 * HALF]
    q0 = 4 * HALF
    q_ref[0] = (_rope(p[:, q0:q0 + HALF], cos_ref, sin_ref) * Q_SCALE).astype(BF16)
    k_ref[0] = _rope(p[:, q0 + HALF:q0 + HALF + KV_WIDTH], cos_ref, sin_ref).astype(BF16)
    v_ref[0] = p[:, q0 + HALF + KV_WIDTH:q0 + HALF + 2 * KV_WIDTH].astype(BF16)
    sag_ref[0] = jax.nn.silu(p[:, q0 + HALF + 2 * KV_WIDTH:])


def _even_out_odd_in(y, sgb, ya, x, mod_e, w_out, mod_o, norm_g, w_in, cos_t, sin_t, *, layer, mod_row, tl):
    b, l, _ = x.shape
    _, _, gt = _mod_specs(layer, mod_row)
    sh, sc, _ = _mod_specs(layer, mod_row)
    tile = lambda w: pl.BlockSpec((1, tl, w), lambda bi, t: (bi, t, 0))
    const = functools.partial(_layer_spec, layer)
    return pl.pallas_call(
        _even_out_odd_in_kernel,
        grid=(b, l // tl),
        in_specs=[tile(HALF), tile(HALF), tile(HALF), tile(D_MODEL), gt, const((D_MODEL, D_MODEL)),
                  sh, sc, const((1, D_MODEL)), const((D_MODEL, ODD_IN)),
                  pl.BlockSpec((tl, 128), lambda bi, t: (t, 0)),
                  pl.BlockSpec((tl, 128), lambda bi, t: (t, 0))],
        out_specs=[tile(D_MODEL), tile(HALF), tile(HALF), tile(HALF), tile(KV_WIDTH), tile(KV_WIDTH), tile(HALF)],
        out_shape=[jax.ShapeDtypeStruct((b, l, D_MODEL), F32),
                   jax.ShapeDtypeStruct((b, l, HALF), F32), jax.ShapeDtypeStruct((b, l, HALF), F32),
                   jax.ShapeDtypeStruct((b, l, HALF), BF16), jax.ShapeDtypeStruct((b, l, KV_WIDTH), BF16),
                   jax.ShapeDtypeStruct((b, l, KV_WIDTH), BF16), jax.ShapeDtypeStruct((b, l, HALF), F32)],
        compiler_params=_params("arbitrary", "arbitrary"),
        name="even_out_odd_in",
    )(y, sgb, ya, x, mod_e, w_out, mod_o, mod_o, norm_g, w_in, cos_t, sin_t)


def _odd_out_kernel(*refs, layer, tq, nt, window, final):
    refs = list(refs)
    q_ref = refs.pop(0)
    if window:
        kp_ref, km_ref, kn_ref, vp_ref, vm_ref, vn_ref = refs[:6]
        refs = refs[6:]
    kc_ref, vc_ref, sink_ref, zp_ref, zm_ref, zn_ref, cbg_ref, sag_ref, x_ref, gt_ref, cw_ref, wout_ref = refs[:12]
    refs = refs[12:]
    fn_ref = refs.pop(0) if final else None
    o_ref = refs.pop(0)

    t = pl.program_id(1)
    nqb = tq // BLOCK
    lane = lax.broadcasted_iota(jnp.int32, (BLOCK, 128), 1)
    low = lane < HEAD_DIM
    kc, vc = kc_ref[0], vc_ref[0]
    if window:
        kext = jnp.concatenate([kp_ref[0], km_ref[0], kn_ref[0]], axis=0)
        vext = jnp.concatenate([vp_ref[0], vm_ref[0], vn_ref[0]], axis=0)
        qi = lax.broadcasted_iota(jnp.int32, (BLOCK, 3 * BLOCK), 0)
        kj = lax.broadcasted_iota(jnp.int32, (BLOCK, 3 * BLOCK), 1)
        band = (kj - qi >= 0) & (kj - qi <= 2 * BLOCK)

    att_rows = []
    for qb in range(nqb):
        qg = q_ref[0, qb * BLOCK:(qb + 1) * BLOCK, :]
        zero = jnp.zeros((BLOCK, 128), BF16)
        slots = []
        for g in range(4):
            qq = qg[:, g * 128:(g + 1) * 128]
            slots += [jnp.where(low, qq, zero), jnp.where(low, zero, qq)]
        qs = jnp.concatenate(slots, axis=0)
        s_c = _dot_nt(qs, kc)
        if window:
            s_w = _dot_nt(qs, kext[qb * BLOCK:(qb + 3) * BLOCK])
            valid = band
            if qb == 0:
                valid = valid & ((kj >= BLOCK) | (t > 0))
            if qb == nqb - 1:
                valid = valid & ((kj < 2 * BLOCK) | (t < nt - 1))
        pws, pcs, dens = [], [], []
        for s in range(8):
            sink = sink_ref[layer, s] * LOG2E
            sc_s = s_c[s * BLOCK:(s + 1) * BLOCK]
            m = jnp.maximum(jnp.max(sc_s, axis=-1, keepdims=True), sink)
            if window:
                sw_s = jnp.where(valid, s_w[s * BLOCK:(s + 1) * BLOCK], NEG_INF)
                m = jnp.maximum(m, jnp.max(sw_s, axis=-1, keepdims=True))
            pc = jnp.exp2(sc_s - m)
            den = jnp.exp2(sink - m) + jnp.sum(pc, axis=-1, keepdims=True)
            if window:
                pw = jnp.exp2(sw_s - m)
                den = den + jnp.sum(pw, axis=-1, keepdims=True)
                pws.append(pw.astype(BF16))
            pcs.append(pc.astype(BF16))
            dens.append(den)
        o = _dot(jnp.concatenate(pcs, axis=0), vc)
        if window:
            o = o + _dot(jnp.concatenate(pws, axis=0), vext[qb * BLOCK:(qb + 3) * BLOCK])
        o = o / jnp.concatenate(dens, axis=0)
        att_rows.append(jnp.concatenate(
            [jnp.where(low, o[(2 * g) * BLOCK:(2 * g + 1) * BLOCK], o[(2 * g + 1) * BLOCK:(2 * g + 2) * BLOCK])
             for g in range(4)], axis=1))
    att = jnp.concatenate(att_rows, axis=0) if nqb > 1 else att_rows[0]
    yd = att * sag_ref[0]

    zp = jnp.where(t > 0, zp_ref[0], 0.0)
    zn = jnp.where(t < nt - 1, zn_ref[0], 0.0)
    zext = jnp.concatenate([zp, zm_ref[0], zn], axis=0)
    conv = (zext[C_HALO - 1:C_HALO - 1 + tq] * cw_ref[0:1, :] + zext[C_HALO:C_HALO + tq] * cw_ref[1:2, :]
            + zext[C_HALO + 1:C_HALO + 1 + tq] * cw_ref[2:3, :])
    yc = cbg_ref[0] * conv
    mix = _dot(jnp.concatenate([yc.astype(BF16), yd.astype(BF16)], axis=1), wout_ref[...])
    out = x_ref[0] + gt_ref[...] * mix
    if final:
        ms = jnp.mean(out * out, axis=-1, keepdims=True)
        out = out * lax.rsqrt(ms + EPS) * fn_ref[...]
    o_ref[0] = out


def _odd_out(x, mod, q, k, v, kc, vc, sink, z, cbg, sag, conv_w, w_out, final_norm, *, layer, mod_row, tq, window):
    b, l, _ = x.shape
    nt = l // tq
    nqb = tq // BLOCK
    lc = kc.shape[1]
    _, _, gt = _mod_specs(layer, mod_row)
    tile = lambda w: pl.BlockSpec((1, tq, w), lambda bi, t: (bi, t, 0))
    prev_blk = pl.BlockSpec((1, BLOCK, KV_WIDTH), lambda bi, t: (bi, jnp.maximum(t * nqb - 1, 0), 0))
    next_blk = pl.BlockSpec((1, BLOCK, KV_WIDTH), lambda bi, t: (bi, jnp.minimum((t + 1) * nqb, l // BLOCK - 1), 0))
    zh = tq // C_HALO
    args, specs = [q], [tile(HALF)]
    if window:
        args += [k, k, k, v, v, v]
        specs += [prev_blk, tile(KV_WIDTH), next_blk, prev_blk, tile(KV_WIDTH), next_blk]
    args += [kc, vc, sink, z, z, z, cbg, sag, x, mod, conv_w, w_out]
    specs += [pl.BlockSpec((1, lc, KV_WIDTH), lambda bi, t: (bi, 0, 0)),
              pl.BlockSpec((1, lc, KV_WIDTH), lambda bi, t: (bi, 0, 0)),
              pl.BlockSpec(memory_space=pltpu.SMEM),
              pl.BlockSpec((1, C_HALO, HALF), lambda bi, t: (bi, jnp.maximum(t * zh - 1, 0), 0)),
              tile(HALF),
              pl.BlockSpec((1, C_HALO, HALF), lambda bi, t: (bi, jnp.minimum((t + 1) * zh, l // C_HALO - 1), 0)),
              tile(HALF), tile(HALF), tile(D_MODEL), gt,
              _layer_spec(layer, (3, HALF)), _layer_spec(layer, (D_MODEL, D_MODEL))]
    if final_norm is not None:
        args.append(final_norm)
        specs.append(pl.BlockSpec((1, D_MODEL), lambda bi, t: (0, 0)))
    return pl.pallas_call(
        functools.partial(_odd_out_kernel, layer=layer, tq=tq, nt=nt, window=window, final=final_norm is not None),
        grid=(b, nt),
        in_specs=specs,
        out_specs=tile(D_MODEL),
        out_shape=jax.ShapeDtypeStruct((b, l, D_MODEL), F32),
        compiler_params=_params("arbitrary", "arbitrary"),
        name="odd_out_window" if window else "odd_out_context",
    )(*args)


def _pair_heads_cols(w):
    lead = w.shape[:-1]
    return w.reshape(*lead, 2, 4, HEAD_DIM).swapaxes(-3, -2).reshape(*lead, HALF)


def _rope_tables(l, grid_w):
    nf = HEAD_DIM // 4
    pos = np.arange(l)
    inv = np.power(ROPE_BASE, -np.arange(nf, dtype=np.float64) / nf)
    ang = np.stack([(pos // grid_w)[:, None] * inv, (pos % grid_w)[:, None] * inv], axis=1)
    cos, sin = np.cos(ang), np.sin(ang)
    cos_t = np.stack([cos, cos], axis=2).reshape(l, HEAD_DIM)
    sin_t = np.stack([-sin, sin], axis=2).reshape(l, HEAD_DIM)
    return jnp.asarray(np.tile(cos_t, (1, 2)), F32), jnp.asarray(np.tile(sin_t, (1, 2)), F32)


X_TILE = 512
GRID_W = 64


def kernel(x, c, ctx, c_ctx, e_norm, e_w_mod, e_b_mod, e_w_in, e_a_conv_w, e_a_conv_b, e_a_ln_g, e_a_ln_b,
           e_a_pw_w, e_a_pw_b, e_b_fw, e_w_out, o_norm, o_w_mod, o_b_mod, o_w_in, o_c_conv_w, o_sink, o_w_out,
           final_norm):
    b, l, _ = x.shape
    lc = ctx.shape[1]
    assert x.shape == (8, 4096, D_MODEL) and ctx.shape == (8, 256, D_MODEL)
    depth = e_norm.shape[0] + o_norm.shape[0]
    assert depth % 2 == 0

    cv = jnp.zeros((MOD_ROWS, D_MODEL), F32).at[:b].set(c).at[CTX_ROW].set(c_ctx)
    mod_e = _modulation(cv, e_w_mod, e_b_mod)
    mod_o = _modulation(cv, o_w_mod, o_b_mod)
    gmat = _fourier_channel_mats(e_b_fw)

    cos_x, sin_x = _rope_tables(l, GRID_W)
    cos_c, sin_c = jnp.ones((lc, 128), F32), jnp.zeros((lc, 128), F32)

    ne, no = e_norm.shape[0], o_norm.shape[0]
    row = lambda a: a.reshape(a.shape[0], 1, a.shape[-1])
    rep = lambda a: jnp.broadcast_to(a[..., None, :], a.shape[:-1] + (SUBLANES, HALF))
    w_glu = e_w_in[:, :, :2 * HALF].reshape(ne, D_MODEL, 2, HALF // LANES, LANES).swapaxes(2, 3)
    we_in = jnp.concatenate([w_glu.reshape(ne, D_MODEL, 2 * HALF), e_w_in[:, :, 2 * HALF:]],
                            axis=2).astype(BF16)
    front = (row(e_norm), we_in, rep(e_a_conv_w), rep(e_a_conv_b), row(e_a_ln_g), row(e_a_ln_b),
             e_a_pw_w.astype(BF16), row(e_a_pw_b), gmat)
    we_out = e_w_out.astype(BF16)
    wo_in = jnp.concatenate([o_w_in[:, :, :4 * HALF], _pair_heads_cols(o_w_in[:, :, 4 * HALF:5 * HALF]),
                             o_w_in[:, :, 5 * HALF:5 * HALF + 2 * KV_WIDTH],
                             _pair_heads_cols(o_w_in[:, :, 5 * HALF + 2 * KV_WIDTH:])], axis=2).astype(BF16)
    wo_att = o_w_out[:, HALF:].reshape(no, 2, 4, HEAD_DIM, D_MODEL).swapaxes(1, 2).reshape(no, HALF, D_MODEL)
    wo_out = jnp.concatenate([o_w_out[:, :HALF], wo_att], axis=1).astype(BF16)
    sink = o_sink.reshape(no, 2, 4).swapaxes(1, 2).reshape(no, N_Q_HEADS)
    norm_o = row(o_norm)
    fin = final_norm.reshape(1, D_MODEL)

    xc = ctx
    for i in range(depth // 2):
        last = i == depth // 2 - 1
        ya, y, sgb = _even_front(xc, mod_e, *front, layer=i, mod_row=CTX_ROW, tl=lc)
        xc, cbg_c, z_c, q_c, k_c, v_c, sag_c = _even_out_odd_in(
            y, sgb, ya, xc, mod_e, we_out, mod_o, norm_o, wo_in, cos_c, sin_c, layer=i, mod_row=CTX_ROW, tl=lc)
        ya, y, sgb = _even_front(x, mod_e, *front, layer=i, mod_row=None, tl=X_TILE)
        x, cbg, z, q, k, v, sag = _even_out_odd_in(
            y, sgb, ya, x, mod_e, we_out, mod_o, norm_o, wo_in, cos_x, sin_x, layer=i, mod_row=None, tl=X_TILE)
        x = _odd_out(x, mod_o, q, k, v, k_c, v_c, sink, z, cbg, sag, o_c_conv_w, wo_out, fin if last else None,
                     layer=i, mod_row=None, tq=X_TILE, window=True)
        if not last:
            xc = _odd_out(xc, mod_o, q_c, None, None, k_c, v_c, sink, z_c, cbg_c, sag_c, o_c_conv_w, wo_out, None,
                          layer=i, mod_row=CTX_ROW, tq=lc, window=False)
    return x
```

```python
import functools

import numpy as np
import jax
import jax.numpy as jnp
from jax import lax
from jax.experimental import pallas as pl
from jax.experimental.pallas import tpu as pltpu

D_MODEL = 1024
HALF = 512
A_CONV_W = 31
A_HALO = 16
C_HALO = 8
SUBLANES = 8
B_GROUPS = 4
B_GROUP_DIM = 128
HEAD_DIM = 64
N_Q_HEADS = 8
KV_WIDTH = 128
BLOCK = 128
ODD_IN = 3328
EVEN_IN = 2560
ROPE_BASE = 10000.0
NEG_INF = -1e30
LOG2E = 1.4426950408889634
Q_SCALE = HEAD_DIM ** -0.5 * LOG2E
EPS = 1e-6
RADIX = 16
MOD_ROWS = 16
CTX_ROW = 8
LANES = 128
MXU_COLS = 256
MXU_ROWS = 16
FFT_COLS = MXU_COLS
PM_SPLIT = ((), (0,), (1, 2), (3,))
PM_TAIL = (4, 5)
PM_ORDER = tuple(i for grp in PM_SPLIT for i in grp) + PM_TAIL
F32 = jnp.float32
BF16 = jnp.bfloat16
VMEM_LIMIT_BYTES = 48 * 1024 * 1024


def _params(*sem):
    return pltpu.CompilerParams(dimension_semantics=sem, vmem_limit_bytes=VMEM_LIMIT_BYTES)


def _dot(a, b):
    return jnp.dot(a, b, preferred_element_type=F32)


def _dot_nt(a, b):
    return lax.dot_general(a, b, (((1,), (1,)), ((), ())), preferred_element_type=F32)


def _pre_norm(x, g_ref, sh_ref, sc_ref):
    mult = g_ref[...] * (1.0 + sc_ref[...])
    ms = jnp.mean(x * x, axis=-1, keepdims=True)
    return (x * lax.rsqrt(ms + EPS)) * mult + sh_ref[...]


def _layer_spec(layer, shape):
    return pl.BlockSpec((None,) + tuple(shape), lambda bi, t: (layer,) + (0,) * len(shape))


def _mod_kernel(c_ref, w_ref, b_ref, o_ref):
    o_ref[0, :, 0, :] = _dot(jax.nn.silu(c_ref[...]), w_ref[0]) + b_ref[0]


def _modulation(cv, w_mod, b_mod):
    n = w_mod.shape[0]
    return pl.pallas_call(
        _mod_kernel,
        grid=(n, 3),
        in_specs=[pl.BlockSpec((MOD_ROWS, D_MODEL), lambda l, j: (0, 0)),
                  pl.BlockSpec((1, D_MODEL, D_MODEL), lambda l, j: (l, 0, j)),
                  pl.BlockSpec((1, 1, D_MODEL), lambda l, j: (l, 0, j))],
        out_specs=pl.BlockSpec((1, MOD_ROWS, 1, D_MODEL), lambda l, j: (l, 0, 0, j)),
        out_shape=jax.ShapeDtypeStruct((n, MOD_ROWS, 1, 3 * D_MODEL), F32),
        compiler_params=_params("arbitrary", "arbitrary"),
        name="modulation",
    )(cv, w_mod, b_mod.reshape(n, 1, 3 * D_MODEL))


def _mod_specs(layer, row):
    def spec(j):
        if row is None:
            return pl.BlockSpec((None, None, 1, D_MODEL), lambda b, t: (layer, b, 0, j))
        return pl.BlockSpec((None, None, 1, D_MODEL), lambda b, t: (layer, row, 0, j))
    return spec(0), spec(1), spec(2)


def _gprep_kernel(cd_ref, sd_ref, fw_ref, o_ref):
    o_ref[...] = jnp.zeros(o_ref.shape, o_ref.dtype)
    for g in range(B_GROUPS):
        fw = fw_ref[0, g]
        gc = jnp.dot(cd_ref[...], fw, preferred_element_type=F32, precision=lax.Precision.HIGHEST)
        gs = jnp.dot(sd_ref[...], fw, preferred_element_type=F32, precision=lax.Precision.HIGHEST)
        lo, hi = g * B_GROUP_DIM, (g + 1) * B_GROUP_DIM
        o_ref[0, lo:hi, lo:hi] = gc.astype(BF16)
        o_ref[0, lo:hi, HALF + lo:HALF + hi] = (-gs).astype(BF16)


def _fourier_channel_mats(fw):
    n = fw.shape[0]
    d = np.arange(B_GROUP_DIM)
    ang = 2.0 * np.pi * ((d[:, None] * d[None, :]) % B_GROUP_DIM) / B_GROUP_DIM
    cd = jnp.asarray(np.cos(ang) / np.sqrt(B_GROUP_DIM), F32)
    sd = jnp.asarray(np.sin(ang) / np.sqrt(B_GROUP_DIM), F32)
    mat = pl.BlockSpec((B_GROUP_DIM, B_GROUP_DIM), lambda i: (0, 0))
    return pl.pallas_call(
        _gprep_kernel,
        grid=(n,),
        in_specs=[mat, mat,
                  pl.BlockSpec((1, B_GROUPS, B_GROUP_DIM, B_GROUP_DIM), lambda i: (i, 0, 0, 0))],
        out_specs=pl.BlockSpec((1, HALF, 2 * HALF), lambda i: (i, 0, 0)),
        out_shape=jax.ShapeDtypeStruct((n, HALF, 2 * HALF), BF16),
        compiler_params=_params("arbitrary"),
        name="fourier_channel_mats",
    )(cd, sd, fw)


def _real_stack(m, real_only=False):
    top = np.concatenate([m.real, -m.imag], axis=1)
    if real_only:
        return top
    return np.concatenate([top, np.concatenate([m.imag, m.real], axis=1)], axis=0)


def _fft_tables_4096():
    r = np.arange(RADIX)
    eye = np.eye(RADIX)
    w16 = np.exp(-2j * np.pi * np.outer(r, r) / 16)
    w256 = np.exp(-2j * np.pi * np.outer(r, r) / 256)
    w4096 = np.exp(-2j * np.pi * np.outer(r, r) / 4096)
    ma = np.einsum("ka,kc,cd->kcad", w16, w4096, eye).reshape(256, 256) / 64.0
    mb = np.einsum("kb,kc,cd->kcbd", w16, w256, eye).reshape(256, 256)
    mc = np.einsum("kc,ae->kaec", w16, eye).reshape(256, 256)
    ph = 2.0 * np.pi * np.outer(r, r) / 256
    as_bf16 = lambda m: jnp.asarray(m, F32).astype(BF16)
    return (as_bf16(_real_stack(ma)), as_bf16(_real_stack(mb)), as_bf16(_real_stack(mc, real_only=True)),
            jnp.asarray(np.cos(ph), F32), jnp.asarray(np.sin(ph), F32))


def _fft_4096_kernel(z_ref, ma_ref, mb_ref, mc_ref, tc_ref, ts_ref, o_ref, s_scr):
    ct = z_ref.shape[-1]
    blk = RADIX * RADIX
    for b in range(RADIX):
        a1 = _dot(ma_ref[...], z_ref[0, :, :, b].reshape(2 * blk, ct))
        ar, ai = a1[:blk], a1[blk:]
        out_r, out_i = [], []
        for ka in range(RADIX):
            rr, ii = ar[ka * RADIX:(ka + 1) * RADIX], ai[ka * RADIX:(ka + 1) * RADIX]
            if b * ka == 0:
                out_r.append(rr)
                out_i.append(ii)
            else:
                c, s = tc_ref[b, ka], ts_ref[b, ka]
                out_r.append(rr * c + ii * s)
                out_i.append(ii * c - rr * s)
        s_scr[0, :, b] = jnp.concatenate(out_r, axis=0).astype(BF16).reshape(RADIX, RADIX, ct)
        s_scr[1, :, b] = jnp.concatenate(out_i, axis=0).astype(BF16).reshape(RADIX, RADIX, ct)
    for ka in range(RADIX):
        a2 = _dot(mb_ref[...], s_scr[:, ka].reshape(2 * blk, ct))
        s_scr[:, ka] = a2.astype(BF16).reshape(2, RADIX, RADIX, ct)
    for kb in range(RADIX):
        y = _dot(mc_ref[...], s_scr[:, :, kb].reshape(2 * blk, ct))
        o_ref[0, :, kb] = y.reshape(RADIX, RADIX, ct)


def _fft_4096(z):
    b = z.shape[0]
    ma, mb, mc, tc, ts = _fft_tables_4096()
    blk = RADIX * RADIX
    mat = lambda rows: pl.BlockSpec((rows, 2 * blk), lambda bi, j: (0, 0))
    smem = pl.BlockSpec(memory_space=pltpu.SMEM)
    y = pl.pallas_call(
        _fft_4096_kernel,
        grid=(b, HALF // FFT_COLS),
        in_specs=[pl.BlockSpec((1, 2, RADIX, RADIX, RADIX, FFT_COLS), lambda bi, j: (bi, 0, 0, 0, 0, j)),
                  mat(2 * blk), mat(2 * blk), mat(blk), smem, smem],
        out_specs=pl.BlockSpec((1, RADIX, RADIX, RADIX, FFT_COLS), lambda bi, j: (bi, 0, 0, 0, j)),
        out_shape=jax.ShapeDtypeStruct((b, RADIX, RADIX, RADIX, HALF), F32),
        scratch_shapes=[pltpu.VMEM((2, RADIX, RADIX, RADIX, FFT_COLS), BF16)],
        compiler_params=_params("arbitrary", "arbitrary"),
        name="fft_4096",
    )(z.reshape(b, 2, RADIX, RADIX, RADIX, HALF), ma, mb, mc, tc, ts)
    return y.reshape(b, RADIX ** 3, HALF)


def _fft_dense_kernel(z_ref, f_ref, o_ref):
    o_ref[0] = _dot(f_ref[...], z_ref[0].reshape(f_ref.shape[1], z_ref.shape[-1]))


def _fft_dense(z):
    b, _, l, _ = z.shape
    k = np.arange(l)
    ang = 2.0 * np.pi * ((k[:, None] * k[None, :]) % l) / l
    f = jnp.asarray(np.concatenate([np.cos(ang), np.sin(ang)], axis=1) / np.sqrt(l), F32).astype(BF16)
    return pl.pallas_call(
        _fft_dense_kernel,
        grid=(b, HALF // FFT_COLS),
        in_specs=[pl.BlockSpec((1, 2, l, FFT_COLS), lambda bi, j: (bi, 0, 0, j)),
                  pl.BlockSpec((l, 2 * l), lambda bi, j: (0, 0))],
        out_specs=pl.BlockSpec((1, l, FFT_COLS), lambda bi, j: (bi, 0, j)),
        out_shape=jax.ShapeDtypeStruct((b, l, HALF), F32),
        compiler_params=_params("arbitrary", "arbitrary"),
        name="fft_dense",
    )(z, f)


def _zeros_once_computed(tiles, zero, rows):
    acc = None
    for tile in tiles:
        bits = lax.bitcast_convert_type(tile, jnp.int32)
        bits = bits.reshape(tile.shape[0] // SUBLANES, SUBLANES, tile.shape[1])
        for i in range(0, bits.shape[0], MXU_ROWS // SUBLANES):
            for c in range(0, tile.shape[1], MXU_COLS):
                part = bits[i, :, c:c + LANES]
                acc = part if acc is None else acc | part
    z = lax.bitcast_convert_type(acc & zero, F32)
    return jnp.broadcast_to(z[None], (rows // SUBLANES, SUBLANES, LANES)).reshape(rows, LANES)


def _even_in_kernel(zero_ref, xp_ref, xm_ref, xn_ref, sh_ref, sc_ref, g_ref, win_ref, cw_ref, cb_ref, lg_ref, lb_ref,
                    pw_ref, pb_ref, gm_ref, ya_ref, z_ref, sgb_ref, u_scr, c_scr, *, tl, nt):
    t = pl.program_id(1)
    ext = tl + 2 * A_HALO
    xe = jnp.concatenate([xp_ref[0], xm_ref[0], xn_ref[0]], axis=0)
    he = _pre_norm(xe, g_ref, sh_ref, sc_ref).astype(BF16)
    row = lax.broadcasted_iota(jnp.int32, (ext, 1), 0)
    inside = ((row >= A_HALO) | (t > 0)) & ((row < tl + A_HALO) | (t < nt - 1))
    sub = lax.broadcasted_iota(jnp.int32, (ext // SUBLANES, SUBLANES, LANES), 1)
    rc = 64
    off = A_HALO - (A_CONV_W - 1) // 2
    hm = he[A_HALO:A_HALO + tl]
    pm_tiles = []
    pad = jnp.zeros((ext, LANES), F32)
    for blk in range(HALF // LANES):
        lanes = slice(blk * LANES, (blk + 1) * LANES)
        w_blk = jnp.concatenate([win_ref[:, lanes], win_ref[:, HALF + blk * LANES:HALF + (blk + 1) * LANES]], axis=1)
        pa = _dot(he, w_blk)
        u = jnp.where(inside, pa[:, :LANES] * jax.nn.sigmoid(pa[:, LANES:]), pad)
        if PM_SPLIT[blk]:
            new = [_dot(hm, win_ref[:, 2 * HALF + i * MXU_COLS:2 * HALF + (i + 1) * MXU_COLS])
                   for i in PM_SPLIT[blk]]
            pm_tiles += new
            pad = _zeros_once_computed(new, zero_ref[0], ext)
        u_scr[0, :, lanes] = u
        u3 = u.reshape(ext // SUBLANES, SUBLANES, LANES)
        for r in range(1, SUBLANES):
            rot = pltpu.roll(u3, SUBLANES - r, 1)
            nxt = jnp.concatenate([rot[1:], rot[:1]], axis=0)
            u_scr[r, :, lanes] = jnp.where(sub < SUBLANES - r, rot, nxt).reshape(ext, LANES)
        for r0 in range(0, tl, rc):
            acc = jnp.broadcast_to(cb_ref[:, lanes][None], (rc // SUBLANES, SUBLANES, LANES))
            for j in range(A_CONV_W):
                q, r = divmod(off + j, SUBLANES)
                uj = u_scr[r, r0 + SUBLANES * q:r0 + SUBLANES * q + rc, lanes]
                acc = acc + uj.reshape(rc // SUBLANES, SUBLANES, LANES) * cw_ref[j, :, lanes][None]
            c_scr[r0:r0 + rc, lanes] = acc.reshape(rc, LANES)
    pm_tiles += [_dot(hm, win_ref[:, 2 * HALF + i * MXU_COLS:2 * HALF + (i + 1) * MXU_COLS]) for i in PM_TAIL]
    pm = jnp.concatenate([pm_tiles[PM_ORDER.index(i)] for i in range(len(PM_ORDER))], axis=1)
    pq = _dot(pm[:, HALF:2 * HALF].astype(BF16), gm_ref[...])
    tie = _zeros_once_computed(pm_tiles[-len(PM_TAIL):] + [pq], zero_ref[0], tl)
    cv = c_scr[...] + jnp.concatenate([tie] * (HALF // LANES), axis=1)
    mu = jnp.mean(cv, axis=-1, keepdims=True)
    cen = cv - mu
    var = jnp.mean(cen * cen, axis=-1, keepdims=True)
    ln = cen * lax.rsqrt(var + EPS) * lg_ref[...] + lb_ref[...]
    ya = (_dot(jax.nn.silu(ln).astype(BF16), pw_ref[...]) + pb_ref[...]) * jax.nn.silu(pm[:, :HALF])
    ya_ref[0] = ya.astype(BF16)
    z_ref[0, 0] = pq[:, :HALF].astype(BF16)
    z_ref[0, 1] = pq[:, HALF:].astype(BF16)
    sgb_ref[0] = jax.nn.silu(pm[:, 2 * HALF:])


def _even_front(x, mod, norm_g, w_in, conv_w, conv_b, ln_g, ln_b, pw_w, pw_b, gmat, *, layer, mod_row, tl):
    b, l, _ = x.shape
    nt = l // tl
    hb = tl // A_HALO
    sh, sc, _ = _mod_specs(layer, mod_row)
    const = functools.partial(_layer_spec, layer)
    tile = lambda w: pl.BlockSpec((1, tl, w), lambda bi, t: (bi, t, 0))
    ya, z, sgb = pl.pallas_call(
        functools.partial(_even_in_kernel, tl=tl, nt=nt),
        grid=(b, nt),
        in_specs=[pl.BlockSpec(memory_space=pltpu.SMEM),
                  pl.BlockSpec((1, A_HALO, D_MODEL), lambda bi, t: (bi, jnp.maximum(t * hb - 1, 0), 0)),
                  tile(D_MODEL),
                  pl.BlockSpec((1, A_HALO, D_MODEL), lambda bi, t: (bi, jnp.minimum((t + 1) * hb, l // A_HALO - 1), 0)),
                  sh, sc, const((1, D_MODEL)), const((D_MODEL, EVEN_IN)), const((A_CONV_W, SUBLANES, HALF)),
                  const((SUBLANES, HALF)), const((1, HALF)), const((1, HALF)), const((HALF, HALF)), const((1, HALF)),
                  const((HALF, 2 * HALF))],
        out_specs=[tile(HALF),
                   pl.BlockSpec((1, 2, tl, HALF), lambda bi, t: (bi, 0, t, 0)),
                   tile(HALF)],
        out_shape=[jax.ShapeDtypeStruct((b, l, HALF), BF16),
                   jax.ShapeDtypeStruct((b, 2, l, HALF), BF16),
                   jax.ShapeDtypeStruct((b, l, HALF), F32)],
        scratch_shapes=[pltpu.VMEM((SUBLANES, tl + 2 * A_HALO, HALF), F32), pltpu.VMEM((tl, HALF), F32)],
        compiler_params=_params("arbitrary", "arbitrary"),
        name="even_in",
    )(jnp.zeros((1,), jnp.int32), x, x, x, mod, mod, norm_g, w_in, conv_w, conv_b, ln_g, ln_b, pw_w, pw_b, gmat)

    y = _fft_4096(z) if l == RADIX ** 3 else _fft_dense(z)
    return ya, y, sgb


def _rope(t, cos_ref, sin_ref):
    reps = t.shape[1] // 128
    c = jnp.concatenate([cos_ref[...]] * reps, axis=1) if reps > 1 else cos_ref[...]
    s = jnp.concatenate([sin_ref[...]] * reps, axis=1) if reps > 1 else sin_ref[...]
    lane = lax.broadcasted_iota(jnp.int32, t.shape, 1)
    partner = jnp.where((lane & 16) == 0, pltpu.roll(t, t.shape[1] - 16, 1), pltpu.roll(t, 16, 1))
    return t * c + partner * s


def _pair_heads(t):
    n = N_Q_HEADS // 2
    vregs = [t[:, j * LANES:(j + 1) * LANES] for j in range(n)]
    swapped = [pltpu.roll(v, HEAD_DIM, 1) for v in vregs]
    low = lax.broadcasted_iota(jnp.int32, vregs[0].shape, 1) < HEAD_DIM
    out = []
    for g in range(n):
        a, b = g // 2, (g + n) // 2
        out.append(jnp.where(low, vregs[a], swapped[b]) if g % 2 == 0 else jnp.where(low, swapped[a], vregs[b]))
    return jnp.concatenate(out, axis=1)


def _even_out_odd_in_kernel(y_ref, sgb_ref, ya_ref, x_ref, gt_ref, wout_ref, sh_ref, sc_ref, g_ref, win_ref,
                            cos_ref, sin_ref, xo_ref, cbg_ref, z_ref, q_ref, k_ref, v_ref, sag_ref):
    yb = (y_ref[0] * sgb_ref[0]).astype(BF16)
    mix = _dot(jnp.concatenate([ya_ref[0], yb], axis=1), wout_ref[...])
    x = x_ref[0] + gt_ref[...] * mix
    xo_ref[0] = x
    h = _pre_norm(x, g_ref, sh_ref, sc_ref).astype(BF16)
    p = _dot(h, win_ref[...])
    cbg_ref[0] = p[:, 0:HALF] * jax.nn.silu(p[:, 3 * HALF:4 * HALF])
    z_ref[0] = p[:, HALF:2 * HALF] * p[:, 2 * HALF:3 * HALF]
    q0 = 4 * HALF
    q = _rope(p[:, q0:q0 + HALF], cos_ref, sin_ref) * Q_SCALE
    q_ref[0] = _pair_heads(q).astype(BF16)
    k_ref[0] = _rope(p[:, q0 + HALF:q0 + HALF + KV_WIDTH], cos_ref, sin_ref).astype(BF16)
    v_ref[0] = p[:, q0 + HALF + KV_WIDTH:q0 + HALF + 2 * KV_WIDTH].astype(BF16)
    sag_ref[0] = _pair_heads(jax.nn.silu(p[:, q0 + HALF + 2 * KV_WIDTH:]))


def _even_out_odd_in(y, sgb, ya, x, mod_e, w_out, mod_o, norm_g, w_in, cos_t, sin_t, *, layer, mod_row, tl):
    b, l, _ = x.shape
    _, _, gt = _mod_specs(layer, mod_row)
    sh, sc, _ = _mod_specs(layer, mod_row)
    tile = lambda w: pl.BlockSpec((1, tl, w), lambda bi, t: (bi, t, 0))
    const = functools.partial(_layer_spec, layer)
    return pl.pallas_call(
        _even_out_odd_in_kernel,
        grid=(b, l // tl),
        in_specs=[tile(HALF), tile(HALF), tile(HALF), tile(D_MODEL), gt, const((D_MODEL, D_MODEL)),
                  sh, sc, const((1, D_MODEL)), const((D_MODEL, ODD_IN)),
                  pl.BlockSpec((tl, 128), lambda bi, t: (t, 0)),
                  pl.BlockSpec((tl, 128), lambda bi, t: (t, 0))],
        out_specs=[tile(D_MODEL), tile(HALF), tile(HALF), tile(HALF), tile(KV_WIDTH), tile(KV_WIDTH), tile(HALF)],
        out_shape=[jax.ShapeDtypeStruct((b, l, D_MODEL), F32),
                   jax.ShapeDtypeStruct((b, l, HALF), F32), jax.ShapeDtypeStruct((b, l, HALF), F32),
                   jax.ShapeDtypeStruct((b, l, HALF), BF16), jax.ShapeDtypeStruct((b, l, KV_WIDTH), BF16),
                   jax.ShapeDtypeStruct((b, l, KV_WIDTH), BF16), jax.ShapeDtypeStruct((b, l, HALF), F32)],
        compiler_params=_params("arbitrary", "arbitrary"),
        name="even_out_odd_in",
    )(y, sgb, ya, x, mod_e, w_out, mod_o, mod_o, norm_g, w_in, cos_t, sin_t)


def _odd_out_kernel(*refs, layer, tq, nt, window, final):
    refs = list(refs)
    q_ref = refs.pop(0)
    if window:
        kp_ref, km_ref, kn_ref, vp_ref, vm_ref, vn_ref = refs[:6]
        refs = refs[6:]
    kc_ref, vc_ref, sink_ref, zp_ref, zm_ref, zn_ref, cbg_ref, sag_ref, x_ref, gt_ref, cw_ref, wout_ref = refs[:12]
    refs = refs[12:]
    fn_ref = refs.pop(0) if final else None
    o_ref = refs.pop(0)

    t = pl.program_id(1)
    nqb = tq // BLOCK
    lane = lax.broadcasted_iota(jnp.int32, (BLOCK, 128), 1)
    low = lane < HEAD_DIM
    kc, vc = kc_ref[0], vc_ref[0]
    if window:
        kext = jnp.concatenate([kp_ref[0], km_ref[0], kn_ref[0]], axis=0)
        vext = jnp.concatenate([vp_ref[0], vm_ref[0], vn_ref[0]], axis=0)
        qi = lax.broadcasted_iota(jnp.int32, (BLOCK, 3 * BLOCK), 0)
        kj = lax.broadcasted_iota(jnp.int32, (BLOCK, 3 * BLOCK), 1)
        band = (kj - qi >= 0) & (kj - qi <= 2 * BLOCK)

    att_rows = []
    zero = jnp.zeros((BLOCK, LANES), BF16)
    heads_per_kv = N_Q_HEADS // 2
    for qb in range(nqb):
        qg = q_ref[0, qb * BLOCK:(qb + 1) * BLOCK, :]
        slots = []
        for g in range(heads_per_kv):
            qq = qg[:, g * LANES:(g + 1) * LANES]
            slots += [jnp.where(low, qq, zero), jnp.where(low, zero, qq)]
        qs = jnp.concatenate(slots, axis=0)
        s_c = _dot_nt(qs, kc)
        if window:
            s_w = _dot_nt(qs, kext[qb * BLOCK:(qb + 3) * BLOCK])
            valid = band
            if qb == 0:
                valid = valid & ((kj >= BLOCK) | (t > 0))
            if qb == nqb - 1:
                valid = valid & ((kj < 2 * BLOCK) | (t < nt - 1))
        pws, pcs, dens = [], [], []
        for s in range(N_Q_HEADS):
            head = s // 2 + heads_per_kv * (s % 2)
            sink = sink_ref[layer, head] * LOG2E
            sc_s = s_c[s * BLOCK:(s + 1) * BLOCK]
            m = jnp.maximum(jnp.max(sc_s, axis=-1, keepdims=True), sink)
            if window:
                sw_s = jnp.where(valid, s_w[s * BLOCK:(s + 1) * BLOCK], NEG_INF)
                m = jnp.maximum(m, jnp.max(sw_s, axis=-1, keepdims=True))
            pc = jnp.exp2(sc_s - m)
            den = jnp.exp2(sink - m) + jnp.sum(pc, axis=-1, keepdims=True)
            if window:
                pw = jnp.exp2(sw_s - m)
                den = den + jnp.sum(pw, axis=-1, keepdims=True)
                pws.append(pw.astype(BF16))
            pcs.append(pc.astype(BF16))
            dens.append(den)
        o = _dot(jnp.concatenate(pcs, axis=0), vc)
        if window:
            o = o + _dot(jnp.concatenate(pws, axis=0), vext[qb * BLOCK:(qb + 3) * BLOCK])
        o = o / jnp.concatenate(dens, axis=0)
        att_rows.append(jnp.concatenate(
            [jnp.where(low, o[(2 * g) * BLOCK:(2 * g + 1) * BLOCK], o[(2 * g + 1) * BLOCK:(2 * g + 2) * BLOCK])
             for g in range(heads_per_kv)], axis=1))
    att = jnp.concatenate(att_rows, axis=0) if nqb > 1 else att_rows[0]
    yd = att * sag_ref[0]

    zp = jnp.where(t > 0, zp_ref[0], 0.0)
    zn = jnp.where(t < nt - 1, zn_ref[0], 0.0)
    zext = jnp.concatenate([zp, zm_ref[0], zn], axis=0)
    conv = (zext[C_HALO - 1:C_HALO - 1 + tq] * cw_ref[0:1, :] + zext[C_HALO:C_HALO + tq] * cw_ref[1:2, :]
            + zext[C_HALO + 1:C_HALO + 1 + tq] * cw_ref[2:3, :])
    yc = cbg_ref[0] * conv
    order = [g + (N_Q_HEADS // 2) * half for g in range(N_Q_HEADS // 2) for half in range(2)]
    w_mix = jnp.concatenate([wout_ref[0:HALF]] + [wout_ref[HALF + h * HEAD_DIM:HALF + (h + 1) * HEAD_DIM]
                                                   for h in order], axis=0)
    mix = _dot(jnp.concatenate([yc.astype(BF16), yd.astype(BF16)], axis=1), w_mix)
    out = x_ref[0] + gt_ref[...] * mix
    if final:
        ms = jnp.mean(out * out, axis=-1, keepdims=True)
        out = out * lax.rsqrt(ms + EPS) * fn_ref[...]
    o_ref[0] = out


def _odd_out(x, mod, q, k, v, kc, vc, sink, z, cbg, sag, conv_w, w_out, final_norm, *, layer, mod_row, tq, window):
    b, l, _ = x.shape
    nt = l // tq
    nqb = tq // BLOCK
    lc = kc.shape[1]
    _, _, gt = _mod_specs(layer, mod_row)
    tile = lambda w: pl.BlockSpec((1, tq, w), lambda bi, t: (bi, t, 0))
    prev_blk = pl.BlockSpec((1, BLOCK, KV_WIDTH), lambda bi, t: (bi, jnp.maximum(t * nqb - 1, 0), 0))
    next_blk = pl.BlockSpec((1, BLOCK, KV_WIDTH), lambda bi, t: (bi, jnp.minimum((t + 1) * nqb, l // BLOCK - 1), 0))
    zh = tq // C_HALO
    args, specs = [q], [tile(HALF)]
    if window:
        args += [k, k, k, v, v, v]
        specs += [prev_blk, tile(KV_WIDTH), next_blk, prev_blk, tile(KV_WIDTH), next_blk]
    args += [kc, vc, sink, z, z, z, cbg, sag, x, mod, conv_w, w_out]
    specs += [pl.BlockSpec((1, lc, KV_WIDTH), lambda bi, t: (bi, 0, 0)),
              pl.BlockSpec((1, lc, KV_WIDTH), lambda bi, t: (bi, 0, 0)),
              pl.BlockSpec(memory_space=pltpu.SMEM),
              pl.BlockSpec((1, C_HALO, HALF), lambda bi, t: (bi, jnp.maximum(t * zh - 1, 0), 0)),
              tile(HALF),
              pl.BlockSpec((1, C_HALO, HALF), lambda bi, t: (bi, jnp.minimum((t + 1) * zh, l // C_HALO - 1), 0)),
              tile(HALF), tile(HALF), tile(D_MODEL), gt,
              _layer_spec(layer, (3, HALF)), _layer_spec(layer, (D_MODEL, D_MODEL))]
    if final_norm is not None:
        args.append(final_norm)
        specs.append(pl.BlockSpec((1, D_MODEL), lambda bi, t: (0, 0)))
    return pl.pallas_call(
        functools.partial(_odd_out_kernel, layer=layer, tq=tq, nt=nt, window=window, final=final_norm is not None),
        grid=(b, nt),
        in_specs=specs,
        out_specs=tile(D_MODEL),
        out_shape=jax.ShapeDtypeStruct((b, l, D_MODEL), F32),
        compiler_params=_params("arbitrary", "arbitrary"),
        name="odd_out_window" if window else "odd_out_context",
    )(*args)


def _rope_tables(l, grid_w):
    nf = HEAD_DIM // 4
    pos = np.arange(l)
    inv = np.power(ROPE_BASE, -np.arange(nf, dtype=np.float64) / nf)
    ang = np.stack([(pos // grid_w)[:, None] * inv, (pos % grid_w)[:, None] * inv], axis=1)
    cos, sin = np.cos(ang), np.sin(ang)
    cos_t = np.stack([cos, cos], axis=2).reshape(l, HEAD_DIM)
    sin_t = np.stack([-sin, sin], axis=2).reshape(l, HEAD_DIM)
    return jnp.asarray(np.tile(cos_t, (1, 2)), F32), jnp.asarray(np.tile(sin_t, (1, 2)), F32)


X_TILE = 512
GRID_W = 64


def kernel(x, c, ctx, c_ctx, e_norm, e_w_mod, e_b_mod, e_w_in, e_a_conv_w, e_a_conv_b, e_a_ln_g, e_a_ln_b,
           e_a_pw_w, e_a_pw_b, e_b_fw, e_w_out, o_norm, o_w_mod, o_b_mod, o_w_in, o_c_conv_w, o_sink, o_w_out,
           final_norm):
    b, l, _ = x.shape
    lc = ctx.shape[1]
    assert x.shape == (8, 4096, D_MODEL) and ctx.shape == (8, 256, D_MODEL)
    depth = e_norm.shape[0] + o_norm.shape[0]
    assert depth % 2 == 0

    cv = jnp.zeros((MOD_ROWS, D_MODEL), F32).at[:b].set(c).at[CTX_ROW].set(c_ctx)
    mod_e = _modulation(cv, e_w_mod, e_b_mod)
    mod_o = _modulation(cv, o_w_mod, o_b_mod)
    gmat = _fourier_channel_mats(e_b_fw)

    cos_x, sin_x = _rope_tables(l, GRID_W)
    cos_c, sin_c = jnp.ones((lc, 128), F32), jnp.zeros((lc, 128), F32)

    row = lambda a: a.reshape(a.shape[0], 1, a.shape[-1])
    rep = lambda a: jnp.broadcast_to(a[..., None, :], a.shape[:-1] + (SUBLANES, HALF))
    front = (row(e_norm), e_w_in.astype(BF16), rep(e_a_conv_w), rep(e_a_conv_b), row(e_a_ln_g), row(e_a_ln_b),
             e_a_pw_w.astype(BF16), row(e_a_pw_b), gmat)
    we_out, wo_in, wo_out = e_w_out.astype(BF16), o_w_in.astype(BF16), o_w_out.astype(BF16)
    sink = o_sink
    norm_o = row(o_norm)
    fin = final_norm.reshape(1, D_MODEL)

    xc = ctx
    for i in range(depth // 2):
        last = i == depth // 2 - 1
        ya, y, sgb = _even_front(xc, mod_e, *front, layer=i, mod_row=CTX_ROW, tl=lc)
        xc, cbg_c, z_c, q_c, k_c, v_c, sag_c = _even_out_odd_in(
            y, sgb, ya, xc, mod_e, we_out, mod_o, norm_o, wo_in, cos_c, sin_c, layer=i, mod_row=CTX_ROW, tl=lc)
        ya, y, sgb = _even_front(x, mod_e, *front, layer=i, mod_row=None, tl=X_TILE)
        x, cbg, z, q, k, v, sag = _even_out_odd_in(
            y, sgb, ya, x, mod_e, we_out, mod_o, norm_o, wo_in, cos_x, sin_x, layer=i, mod_row=None, tl=X_TILE)
        x = _odd_out(x, mod_o, q, k, v, k_c, v_c, sink, z, cbg, sag, o_c_conv_w, wo_out, fin if last else None,
                     layer=i, mod_row=None, tq=X_TILE, window=True)
        if not last:
            xc = _odd_out(xc, mod_o, q_c, None, None, k_c, v_c, sink, z_c, cbg_c, sag_c, o_c_conv_w, wo_out, None,
                          layer=i, mod_row=CTX_ROW, tq=lc, window=False)
    return x
```

```python
import functools

import numpy as np
import jax
import jax.numpy as jnp
from jax import lax
from jax.experimental import pallas as pl
from jax.experimental.pallas import tpu as pltpu

D_MODEL = 1024
HALF = 512
A_CONV_W = 31
A_HALO = 16
C_HALO = 8
SUBLANES = 8
B_GROUPS = 4
B_GROUP_DIM = 128
HEAD_DIM = 64
N_Q_HEADS = 8
KV_WIDTH = 128
BLOCK = 128
ODD_IN = 3328
EVEN_IN = 2560
ROPE_BASE = 10000.0
NEG_INF = -1e30
LOG2E = 1.4426950408889634
Q_SCALE = HEAD_DIM ** -0.5 * LOG2E
EPS = 1e-6
RADIX = 16
MOD_ROWS = 16
CTX_ROW = 8
LANES = 128
MXU_COLS = 256
MXU_ROWS = 16
FFT_COLS = MXU_COLS
PM_SPLIT = ((), (0,), (1, 2), (3,))
PM_TAIL = (4, 5)
PM_ORDER = tuple(i for grp in PM_SPLIT for i in grp) + PM_TAIL
F32 = jnp.float32
BF16 = jnp.bfloat16
VMEM_LIMIT_BYTES = 48 * 1024 * 1024


def _params(*sem):
    return pltpu.CompilerParams(dimension_semantics=sem, vmem_limit_bytes=VMEM_LIMIT_BYTES)


def _dot(a, b):
    return jnp.dot(a, b, preferred_element_type=F32)


def _dot_nt(a, b):
    return lax.dot_general(a, b, (((1,), (1,)), ((), ())), preferred_element_type=F32)


def _pre_norm(x, g_ref, sh_ref, sc_ref):
    mult = g_ref[...] * (1.0 + sc_ref[...])
    ms = jnp.mean(x * x, axis=-1, keepdims=True)
    return (x * lax.rsqrt(ms + EPS)) * mult + sh_ref[...]


def _layer_spec(layer, shape):
    return pl.BlockSpec((None,) + tuple(shape), lambda bi, t: (layer,) + (0,) * len(shape))


def _mod_kernel(c_ref, w_ref, b_ref, o_ref):
    o_ref[0, :, 0, :] = _dot(jax.nn.silu(c_ref[...]), w_ref[0]) + b_ref[0]


def _modulation(cv, w_mod, b_mod):
    n = w_mod.shape[0]
    return pl.pallas_call(
        _mod_kernel,
        grid=(n, 3),
        in_specs=[pl.BlockSpec((MOD_ROWS, D_MODEL), lambda l, j: (0, 0)),
                  pl.BlockSpec((1, D_MODEL, D_MODEL), lambda l, j: (l, 0, j)),
                  pl.BlockSpec((1, 1, D_MODEL), lambda l, j: (l, 0, j))],
        out_specs=pl.BlockSpec((1, MOD_ROWS, 1, D_MODEL), lambda l, j: (l, 0, 0, j)),
        out_shape=jax.ShapeDtypeStruct((n, MOD_ROWS, 1, 3 * D_MODEL), F32),
        compiler_params=_params("arbitrary", "arbitrary"),
        name="modulation",
    )(cv, w_mod, b_mod.reshape(n, 1, 3 * D_MODEL))


def _mod_specs(layer, row):
    def spec(j):
        if row is None:
            return pl.BlockSpec((None, None, 1, D_MODEL), lambda b, t: (layer, b, 0, j))
        return pl.BlockSpec((None, None, 1, D_MODEL), lambda b, t: (layer, row, 0, j))
    return spec(0), spec(1), spec(2)


def _gprep_kernel(cd_ref, sd_ref, fw_ref, o_ref):
    o_ref[...] = jnp.zeros(o_ref.shape, o_ref.dtype)
    for g in range(B_GROUPS):
        fw = fw_ref[0, g]
        gc = jnp.dot(cd_ref[...], fw, preferred_element_type=F32, precision=lax.Precision.HIGHEST)
        gs = jnp.dot(sd_ref[...], fw, preferred_element_type=F32, precision=lax.Precision.HIGHEST)
        lo, hi = g * B_GROUP_DIM, (g + 1) * B_GROUP_DIM
        o_ref[0, lo:hi, lo:hi] = gc.astype(BF16)
        o_ref[0, lo:hi, HALF + lo:HALF + hi] = (-gs).astype(BF16)


def _fourier_channel_mats(fw):
    n = fw.shape[0]
    d = np.arange(B_GROUP_DIM)
    ang = 2.0 * np.pi * ((d[:, None] * d[None, :]) % B_GROUP_DIM) / B_GROUP_DIM
    cd = jnp.asarray(np.cos(ang) / np.sqrt(B_GROUP_DIM), F32)
    sd = jnp.asarray(np.sin(ang) / np.sqrt(B_GROUP_DIM), F32)
    mat = pl.BlockSpec((B_GROUP_DIM, B_GROUP_DIM), lambda i: (0, 0))
    return pl.pallas_call(
        _gprep_kernel,
        grid=(n,),
        in_specs=[mat, mat,
                  pl.BlockSpec((1, B_GROUPS, B_GROUP_DIM, B_GROUP_DIM), lambda i: (i, 0, 0, 0))],
        out_specs=pl.BlockSpec((1, HALF, 2 * HALF), lambda i: (i, 0, 0)),
        out_shape=jax.ShapeDtypeStruct((n, HALF, 2 * HALF), BF16),
        compiler_params=_params("arbitrary"),
        name="fourier_channel_mats",
    )(cd, sd, fw)


def _real_stack(m, real_only=False):
    top = np.concatenate([m.real, -m.imag], axis=1)
    if real_only:
        return top
    return np.concatenate([top, np.concatenate([m.imag, m.real], axis=1)], axis=0)


def _fft_tables_4096():
    r = np.arange(RADIX)
    eye = np.eye(RADIX)
    w16 = np.exp(-2j * np.pi * np.outer(r, r) / 16)
    w256 = np.exp(-2j * np.pi * np.outer(r, r) / 256)
    w4096 = np.exp(-2j * np.pi * np.outer(r, r) / 4096)
    ma = np.einsum("ka,kc,cd->kcad", w16, w4096, eye).reshape(256, 256) / 64.0
    mb = np.einsum("kb,kc,cd->kcbd", w16, w256, eye).reshape(256, 256)
    mc = np.einsum("kc,ae->kaec", w16, eye).reshape(256, 256)
    ph = 2.0 * np.pi * np.outer(r, r) / 256
    as_bf16 = lambda m: jnp.asarray(m, F32).astype(BF16)
    return (as_bf16(_real_stack(ma)), as_bf16(_real_stack(mb)), as_bf16(_real_stack(mc, real_only=True)),
            jnp.asarray(np.cos(ph), F32), jnp.asarray(np.sin(ph), F32))


def _fft_4096_kernel(z_ref, ma_ref, mb_ref, mc_ref, tc_ref, ts_ref, o_ref, s_scr):
    ct = z_ref.shape[-1]
    blk = RADIX * RADIX
    for b in range(RADIX):
        a1 = _dot(ma_ref[...], z_ref[0, :, :, b].reshape(2 * blk, ct))
        ar, ai = a1[:blk], a1[blk:]
        out_r, out_i = [], []
        for ka in range(RADIX):
            rr, ii = ar[ka * RADIX:(ka + 1) * RADIX], ai[ka * RADIX:(ka + 1) * RADIX]
            if b * ka == 0:
                out_r.append(rr)
                out_i.append(ii)
            else:
                c, s = tc_ref[b, ka], ts_ref[b, ka]
                out_r.append(rr * c + ii * s)
                out_i.append(ii * c - rr * s)
        s_scr[0, :, b] = jnp.concatenate(out_r, axis=0).astype(BF16).reshape(RADIX, RADIX, ct)
        s_scr[1, :, b] = jnp.concatenate(out_i, axis=0).astype(BF16).reshape(RADIX, RADIX, ct)
    for ka in range(RADIX):
        a2 = _dot(mb_ref[...], s_scr[:, ka].reshape(2 * blk, ct))
        s_scr[:, ka] = a2.astype(BF16).reshape(2, RADIX, RADIX, ct)
    for kb in range(RADIX):
        y = _dot(mc_ref[...], s_scr[:, :, kb].reshape(2 * blk, ct))
        o_ref[0, :, kb] = y.reshape(RADIX, RADIX, ct)


def _fft_4096(z):
    b = z.shape[0]
    ma, mb, mc, tc, ts = _fft_tables_4096()
    blk = RADIX * RADIX
    mat = lambda rows: pl.BlockSpec((rows, 2 * blk), lambda bi, j: (0, 0))
    smem = pl.BlockSpec(memory_space=pltpu.SMEM)
    y = pl.pallas_call(
        _fft_4096_kernel,
        grid=(b, HALF // FFT_COLS),
        in_specs=[pl.BlockSpec((1, 2, RADIX, RADIX, RADIX, FFT_COLS), lambda bi, j: (bi, 0, 0, 0, 0, j)),
                  mat(2 * blk), mat(2 * blk), mat(blk), smem, smem],
        out_specs=pl.BlockSpec((1, RADIX, RADIX, RADIX, FFT_COLS), lambda bi, j: (bi, 0, 0, 0, j)),
        out_shape=jax.ShapeDtypeStruct((b, RADIX, RADIX, RADIX, HALF), F32),
        scratch_shapes=[pltpu.VMEM((2, RADIX, RADIX, RADIX, FFT_COLS), BF16)],
        compiler_params=_params("arbitrary", "arbitrary"),
        name="fft_4096",
    )(z.reshape(b, 2, RADIX, RADIX, RADIX, HALF), ma, mb, mc, tc, ts)
    return y.reshape(b, RADIX ** 3, HALF)


def _fft_dense_kernel(z_ref, f_ref, o_ref):
    o_ref[0] = _dot(f_ref[...], z_ref[0].reshape(f_ref.shape[1], z_ref.shape[-1]))


def _fft_dense(z):
    b, _, l, _ = z.shape
    k = np.arange(l)
    ang = 2.0 * np.pi * ((k[:, None] * k[None, :]) % l) / l
    f = jnp.asarray(np.concatenate([np.cos(ang), np.sin(ang)], axis=1) / np.sqrt(l), F32).astype(BF16)
    return pl.pallas_call(
        _fft_dense_kernel,
        grid=(b, HALF // FFT_COLS),
        in_specs=[pl.BlockSpec((1, 2, l, FFT_COLS), lambda bi, j: (bi, 0, 0, j)),
                  pl.BlockSpec((l, 2 * l), lambda bi, j: (0, 0))],
        out_specs=pl.BlockSpec((1, l, FFT_COLS), lambda bi, j: (bi, 0, j)),
        out_shape=jax.ShapeDtypeStruct((b, l, HALF), F32),
        compiler_params=_params("arbitrary", "arbitrary"),
        name="fft_dense",
    )(z, f)


def _zeros_once_computed(tiles, zero, rows):
    acc = None
    for tile in tiles:
        bits = lax.bitcast_convert_type(tile, jnp.int32)
        bits = bits.reshape(tile.shape[0] // SUBLANES, SUBLANES, tile.shape[1])
        for i in range(0, bits.shape[0], MXU_ROWS // SUBLANES):
            for c in range(0, tile.shape[1], MXU_COLS):
                part = bits[i, :, c:c + LANES]
                acc = part if acc is None else acc | part
    z = lax.bitcast_convert_type(acc & zero, F32)
    return jnp.broadcast_to(z[None], (rows // SUBLANES, SUBLANES, LANES)).reshape(rows, LANES)


def _even_in_kernel(zero_ref, xp_ref, xm_ref, xn_ref, sh_ref, sc_ref, g_ref, win_ref, cw_ref, cb_ref, lg_ref, lb_ref,
                    pw_ref, pb_ref, gm_ref, ya_ref, z_ref, sgb_ref, u_scr, c_scr, *, tl, nt):
    t = pl.program_id(1)
    ext = tl + 2 * A_HALO
    xe = jnp.concatenate([xp_ref[0], xm_ref[0], xn_ref[0]], axis=0)
    he = _pre_norm(xe, g_ref, sh_ref, sc_ref).astype(BF16)
    row = lax.broadcasted_iota(jnp.int32, (ext, 1), 0)
    inside = ((row >= A_HALO) | (t > 0)) & ((row < tl + A_HALO) | (t < nt - 1))
    sub = lax.broadcasted_iota(jnp.int32, (ext // SUBLANES, SUBLANES, LANES), 1)
    rc = 128
    off = A_HALO - (A_CONV_W - 1) // 2
    hm = he[A_HALO:A_HALO + tl]
    pm_tiles = []
    pad = jnp.zeros((ext, LANES), F32)
    for blk in range(HALF // LANES):
        lanes = slice(blk * LANES, (blk + 1) * LANES)
        w_blk = jnp.concatenate([win_ref[:, lanes], win_ref[:, HALF + blk * LANES:HALF + (blk + 1) * LANES]], axis=1)
        pa = _dot(he, w_blk)
        u = jnp.where(inside, pa[:, :LANES] * jax.nn.sigmoid(pa[:, LANES:]), pad)
        if PM_SPLIT[blk]:
            new = [_dot(hm, win_ref[:, 2 * HALF + i * MXU_COLS:2 * HALF + (i + 1) * MXU_COLS])
                   for i in PM_SPLIT[blk]]
            pm_tiles += new
            pad = _zeros_once_computed(new, zero_ref[0], ext)
        u_scr[0, :, lanes] = u
        u3 = u.reshape(ext // SUBLANES, SUBLANES, LANES)
        for r in range(1, SUBLANES):
            rot = pltpu.roll(u3, SUBLANES - r, 1)
            nxt = jnp.concatenate([rot[1:], rot[:1]], axis=0)
            u_scr[r, :, lanes] = jnp.where(sub < SUBLANES - r, rot, nxt).reshape(ext, LANES)
        for r0 in range(0, tl, rc):
            acc = jnp.broadcast_to(cb_ref[:, lanes][None], (rc // SUBLANES, SUBLANES, LANES))
            for j in range(A_CONV_W):
                q, r = divmod(off + j, SUBLANES)
                uj = u_scr[r, r0 + SUBLANES * q:r0 + SUBLANES * q + rc, lanes]
                acc = acc + uj.reshape(rc // SUBLANES, SUBLANES, LANES) * cw_ref[j, :, lanes][None]
            c_scr[r0:r0 + rc, lanes] = acc.reshape(rc, LANES)
    pm_tiles += [_dot(hm, win_ref[:, 2 * HALF + i * MXU_COLS:2 * HALF + (i + 1) * MXU_COLS]) for i in PM_TAIL]
    pm = jnp.concatenate([pm_tiles[PM_ORDER.index(i)] for i in range(len(PM_ORDER))], axis=1)
    pq = _dot(pm[:, HALF:2 * HALF].astype(BF16), gm_ref[...])
    tie = _zeros_once_computed(pm_tiles[-len(PM_TAIL):] + [pq], zero_ref[0], tl)
    cv = c_scr[...] + jnp.concatenate([tie] * (HALF // LANES), axis=1)
    mu = jnp.mean(cv, axis=-1, keepdims=True)
    cen = cv - mu
    var = jnp.mean(cen * cen, axis=-1, keepdims=True)
    ln = cen * lax.rsqrt(var + EPS) * lg_ref[...] + lb_ref[...]
    ya = (_dot(jax.nn.silu(ln).astype(BF16), pw_ref[...]) + pb_ref[...]) * jax.nn.silu(pm[:, :HALF])
    ya_ref[0] = ya.astype(BF16)
    z_ref[0, 0] = pq[:, :HALF].astype(BF16)
    z_ref[0, 1] = pq[:, HALF:].astype(BF16)
    sgb_ref[0] = jax.nn.silu(pm[:, 2 * HALF:])


def _even_front(x, mod, norm_g, w_in, conv_w, conv_b, ln_g, ln_b, pw_w, pw_b, gmat, *, layer, mod_row, tl):
    b, l, _ = x.shape
    nt = l // tl
    hb = tl // A_HALO
    sh, sc, _ = _mod_specs(layer, mod_row)
    const = functools.partial(_layer_spec, layer)
    tile = lambda w: pl.BlockSpec((1, tl, w), lambda bi, t: (bi, t, 0))
    ya, z, sgb = pl.pallas_call(
        functools.partial(_even_in_kernel, tl=tl, nt=nt),
        grid=(b, nt),
        in_specs=[pl.BlockSpec(memory_space=pltpu.SMEM),
                  pl.BlockSpec((1, A_HALO, D_MODEL), lambda bi, t: (bi, jnp.maximum(t * hb - 1, 0), 0)),
                  tile(D_MODEL),
                  pl.BlockSpec((1, A_HALO, D_MODEL), lambda bi, t: (bi, jnp.minimum((t + 1) * hb, l // A_HALO - 1), 0)),
                  sh, sc, const((1, D_MODEL)), const((D_MODEL, EVEN_IN)), const((A_CONV_W, SUBLANES, HALF)),
                  const((SUBLANES, HALF)), const((1, HALF)), const((1, HALF)), const((HALF, HALF)), const((1, HALF)),
                  const((HALF, 2 * HALF))],
        out_specs=[tile(HALF),
                   pl.BlockSpec((1, 2, tl, HALF), lambda bi, t: (bi, 0, t, 0)),
                   tile(HALF)],
        out_shape=[jax.ShapeDtypeStruct((b, l, HALF), BF16),
                   jax.ShapeDtypeStruct((b, 2, l, HALF), BF16),
                   jax.ShapeDtypeStruct((b, l, HALF), F32)],
        scratch_shapes=[pltpu.VMEM((SUBLANES, tl + 2 * A_HALO, HALF), F32), pltpu.VMEM((tl, HALF), F32)],
        compiler_params=_params("arbitrary", "arbitrary"),
        name="even_in",
    )(jnp.zeros((1,), jnp.int32), x, x, x, mod, mod, norm_g, w_in, conv_w, conv_b, ln_g, ln_b, pw_w, pw_b, gmat)

    y = _fft_4096(z) if l == RADIX ** 3 else _fft_dense(z)
    return ya, y, sgb


def _rope(t, cos_ref, sin_ref):
    reps = t.shape[1] // 128
    c = jnp.concatenate([cos_ref[...]] * reps, axis=1) if reps > 1 else cos_ref[...]
    s = jnp.concatenate([sin_ref[...]] * reps, axis=1) if reps > 1 else sin_ref[...]
    lane = lax.broadcasted_iota(jnp.int32, t.shape, 1)
    partner = jnp.where((lane & 16) == 0, pltpu.roll(t, t.shape[1] - 16, 1), pltpu.roll(t, 16, 1))
    return t * c + partner * s


def _pair_heads(t):
    n = N_Q_HEADS // 2
    vregs = [t[:, j * LANES:(j + 1) * LANES] for j in range(n)]
    swapped = [pltpu.roll(v, HEAD_DIM, 1) for v in vregs]
    low = lax.broadcasted_iota(jnp.int32, vregs[0].shape, 1) < HEAD_DIM
    out = []
    for g in range(n):
        a, b = g // 2, (g + n) // 2
        out.append(jnp.where(low, vregs[a], swapped[b]) if g % 2 == 0 else jnp.where(low, swapped[a], vregs[b]))
    return jnp.concatenate(out, axis=1)


def _even_out_odd_in_kernel(y_ref, sgb_ref, ya_ref, x_ref, gt_ref, wout_ref, sh_ref, sc_ref, g_ref, win_ref,
                            cos_ref, sin_ref, xo_ref, cbg_ref, z_ref, q_ref, k_ref, v_ref, sag_ref):
    yb = (y_ref[0] * sgb_ref[0]).astype(BF16)
    mix = _dot(jnp.concatenate([ya_ref[0], yb], axis=1), wout_ref[...])
    x = x_ref[0] + gt_ref[...] * mix
    xo_ref[0] = x
    h = _pre_norm(x, g_ref, sh_ref, sc_ref).astype(BF16)
    q0 = 4 * HALF
    pa = _dot(h, win_ref[:, q0:ODD_IN])
    q = _rope(pa[:, 0:HALF], cos_ref, sin_ref) * Q_SCALE
    q_ref[0] = _pair_heads(q).astype(BF16)
    k_ref[0] = _rope(pa[:, HALF:HALF + KV_WIDTH], cos_ref, sin_ref).astype(BF16)
    v_ref[0] = pa[:, HALF + KV_WIDTH:HALF + 2 * KV_WIDTH].astype(BF16)
    sag_ref[0] = _pair_heads(jax.nn.silu(pa[:, HALF + 2 * KV_WIDTH:]))
    pc = _dot(h, win_ref[:, 0:q0])
    cbg_ref[0] = pc[:, 0:HALF] * jax.nn.silu(pc[:, 3 * HALF:4 * HALF])
    z_ref[0] = pc[:, HALF:2 * HALF] * pc[:, 2 * HALF:3 * HALF]


def _even_out_odd_in(y, sgb, ya, x, mod_e, w_out, mod_o, norm_g, w_in, cos_t, sin_t, *, layer, mod_row, tl):
    b, l, _ = x.shape
    _, _, gt = _mod_specs(layer, mod_row)
    sh, sc, _ = _mod_specs(layer, mod_row)
    tile = lambda w: pl.BlockSpec((1, tl, w), lambda bi, t: (bi, t, 0))
    const = functools.partial(_layer_spec, layer)
    return pl.pallas_call(
        _even_out_odd_in_kernel,
        grid=(b, l // tl),
        in_specs=[tile(HALF), tile(HALF), tile(HALF), tile(D_MODEL), gt, const((D_MODEL, D_MODEL)),
                  sh, sc, const((1, D_MODEL)), const((D_MODEL, ODD_IN)),
                  pl.BlockSpec((tl, 128), lambda bi, t: (t, 0)),
                  pl.BlockSpec((tl, 128), lambda bi, t: (t, 0))],
        out_specs=[tile(D_MODEL), tile(HALF), tile(HALF), tile(HALF), tile(KV_WIDTH), tile(KV_WIDTH), tile(HALF)],
        out_shape=[jax.ShapeDtypeStruct((b, l, D_MODEL), F32),
                   jax.ShapeDtypeStruct((b, l, HALF), F32), jax.ShapeDtypeStruct((b, l, HALF), F32),
                   jax.ShapeDtypeStruct((b, l, HALF), BF16), jax.ShapeDtypeStruct((b, l, KV_WIDTH), BF16),
                   jax.ShapeDtypeStruct((b, l, KV_WIDTH), BF16), jax.ShapeDtypeStruct((b, l, HALF), F32)],
        compiler_params=_params("arbitrary", "arbitrary"),
        name="even_out_odd_in",
    )(y, sgb, ya, x, mod_e, w_out, mod_o, mod_o, norm_g, w_in, cos_t, sin_t)


def _odd_out_kernel(*refs, layer, tq, nt, window, final):
    refs = list(refs)
    q_ref = refs.pop(0)
    if window:
        kp_ref, km_ref, kn_ref, vp_ref, vm_ref, vn_ref = refs[:6]
        refs = refs[6:]
    kc_ref, vc_ref, sink_ref, zp_ref, zm_ref, zn_ref, cbg_ref, sag_ref, x_ref, gt_ref, cw_ref, wout_ref = refs[:12]
    refs = refs[12:]
    fn_ref = refs.pop(0) if final else None
    o_ref = refs.pop(0)

    t = pl.program_id(1)
    nqb = tq // BLOCK
    lane = lax.broadcasted_iota(jnp.int32, (BLOCK, 128), 1)
    low = lane < HEAD_DIM
    kc, vc = kc_ref[0], vc_ref[0]
    if window:
        kext = jnp.concatenate([kp_ref[0], km_ref[0], kn_ref[0]], axis=0)
        vext = jnp.concatenate([vp_ref[0], vm_ref[0], vn_ref[0]], axis=0)
        qi = lax.broadcasted_iota(jnp.int32, (BLOCK, 3 * BLOCK), 0)
        kj = lax.broadcasted_iota(jnp.int32, (BLOCK, 3 * BLOCK), 1)
        band = (kj - qi >= 0) & (kj - qi <= 2 * BLOCK)

    att_rows = []
    zero = jnp.zeros((BLOCK, LANES), BF16)
    heads_per_kv = N_Q_HEADS // 2
    for qb in range(nqb):
        qg = q_ref[0, qb * BLOCK:(qb + 1) * BLOCK, :]
        slots = []
        for g in range(heads_per_kv):
            qq = qg[:, g * LANES:(g + 1) * LANES]
            slots += [jnp.where(low, qq, zero), jnp.where(low, zero, qq)]
        qs = jnp.concatenate(slots, axis=0)
        s_c = _dot_nt(qs, kc)
        if window:
            s_w = _dot_nt(qs, kext[qb * BLOCK:(qb + 3) * BLOCK])
            valid = band
            if qb == 0:
                valid = valid & ((kj >= BLOCK) | (t > 0))
            if qb == nqb - 1:
                valid = valid & ((kj < 2 * BLOCK) | (t < nt - 1))
        pws, pcs, dens = [], [], []
        for s in range(N_Q_HEADS):
            head = s // 2 + heads_per_kv * (s % 2)
            sink = sink_ref[layer, head] * LOG2E
            sc_s = s_c[s * BLOCK:(s + 1) * BLOCK]
            m = jnp.maximum(jnp.max(sc_s, axis=-1, keepdims=True), sink)
            if window:
                sw_s = jnp.where(valid, s_w[s * BLOCK:(s + 1) * BLOCK], NEG_INF)
                m = jnp.maximum(m, jnp.max(sw_s, axis=-1, keepdims=True))
            pc = jnp.exp2(sc_s - m)
            den = jnp.exp2(sink - m) + jnp.sum(pc, axis=-1, keepdims=True)
            if window:
                pw = jnp.exp2(sw_s - m)
                den = den + jnp.sum(pw, axis=-1, keepdims=True)
                pws.append(pw.astype(BF16))
            pcs.append(pc.astype(BF16))
            dens.append(den)
        o = _dot(jnp.concatenate(pcs, axis=0), vc)
        if window:
            o = o + _dot(jnp.concatenate(pws, axis=0), vext[qb * BLOCK:(qb + 3) * BLOCK])
        o = o / jnp.concatenate(dens, axis=0)
        att_rows.append(jnp.concatenate(
            [jnp.where(low, o[(2 * g) * BLOCK:(2 * g + 1) * BLOCK], o[(2 * g + 1) * BLOCK:(2 * g + 2) * BLOCK])
             for g in range(heads_per_kv)], axis=1))
    att = jnp.concatenate(att_rows, axis=0) if nqb > 1 else att_rows[0]
    yd = att * sag_ref[0]

    zp = jnp.where(t > 0, zp_ref[0], 0.0)
    zn = jnp.where(t < nt - 1, zn_ref[0], 0.0)
    zext = jnp.concatenate([zp, zm_ref[0], zn], axis=0)
    conv = (zext[C_HALO - 1:C_HALO - 1 + tq] * cw_ref[0:1, :] + zext[C_HALO:C_HALO + tq] * cw_ref[1:2, :]
            + zext[C_HALO + 1:C_HALO + 1 + tq] * cw_ref[2:3, :])
    yc = cbg_ref[0] * conv
    order = [g + (N_Q_HEADS // 2) * half for g in range(N_Q_HEADS // 2) for half in range(2)]
    w_att = jnp.concatenate([wout_ref[HALF + h * HEAD_DIM:HALF + (h + 1) * HEAD_DIM] for h in order], axis=0)
    mix = _dot(yc.astype(BF16), wout_ref[0:HALF]) + _dot(yd.astype(BF16), w_att)
    out = x_ref[0] + gt_ref[...] * mix
    if final:
        ms = jnp.mean(out * out, axis=-1, keepdims=True)
        out = out * lax.rsqrt(ms + EPS) * fn_ref[...]
    o_ref[0] = out


def _odd_out(x, mod, q, k, v, kc, vc, sink, z, cbg, sag, conv_w, w_out, final_norm, *, layer, mod_row, tq, window):
    b, l, _ = x.shape
    nt = l // tq
    nqb = tq // BLOCK
    lc = kc.shape[1]
    _, _, gt = _mod_specs(layer, mod_row)
    tile = lambda w: pl.BlockSpec((1, tq, w), lambda bi, t: (bi, t, 0))
    prev_blk = pl.BlockSpec((1, BLOCK, KV_WIDTH), lambda bi, t: (bi, jnp.maximum(t * nqb - 1, 0), 0))
    next_blk = pl.BlockSpec((1, BLOCK, KV_WIDTH), lambda bi, t: (bi, jnp.minimum((t + 1) * nqb, l // BLOCK - 1), 0))
    zh = tq // C_HALO
    args, specs = [q], [tile(HALF)]
    if window:
        args += [k, k, k, v, v, v]
        specs += [prev_blk, tile(KV_WIDTH), next_blk, prev_blk, tile(KV_WIDTH), next_blk]
    args += [kc, vc, sink, z, z, z, cbg, sag, x, mod, conv_w, w_out]
    specs += [pl.BlockSpec((1, lc, KV_WIDTH), lambda bi, t: (bi, 0, 0)),
              pl.BlockSpec((1, lc, KV_WIDTH), lambda bi, t: (bi, 0, 0)),
              pl.BlockSpec(memory_space=pltpu.SMEM),
              pl.BlockSpec((1, C_HALO, HALF), lambda bi, t: (bi, jnp.maximum(t * zh - 1, 0), 0)),
              tile(HALF),
              pl.BlockSpec((1, C_HALO, HALF), lambda bi, t: (bi, jnp.minimum((t + 1) * zh, l // C_HALO - 1), 0)),
              tile(HALF), tile(HALF), tile(D_MODEL), gt,
              _layer_spec(layer, (3, HALF)), _layer_spec(layer, (D_MODEL, D_MODEL))]
    if final_norm is not None:
        args.append(final_norm)
        specs.append(pl.BlockSpec((1, D_MODEL), lambda bi, t: (0, 0)))
    return pl.pallas_call(
        functools.partial(_odd_out_kernel, layer=layer, tq=tq, nt=nt, window=window, final=final_norm is not None),
        grid=(b, nt),
        in_specs=specs,
        out_specs=tile(D_MODEL),
        out_shape=jax.ShapeDtypeStruct((b, l, D_MODEL), F32),
        compiler_params=_params("arbitrary", "arbitrary"),
        name="odd_out_window" if window else "odd_out_context",
    )(*args)


def _rope_tables(l, grid_w):
    nf = HEAD_DIM // 4
    pos = np.arange(l)
    inv = np.power(ROPE_BASE, -np.arange(nf, dtype=np.float64) / nf)
    ang = np.stack([(pos // grid_w)[:, None] * inv, (pos % grid_w)[:, None] * inv], axis=1)
    cos, sin = np.cos(ang), np.sin(ang)
    cos_t = np.stack([cos, cos], axis=2).reshape(l, HEAD_DIM)
    sin_t = np.stack([-sin, sin], axis=2).reshape(l, HEAD_DIM)
    return jnp.asarray(np.tile(cos_t, (1, 2)), F32), jnp.asarray(np.tile(sin_t, (1, 2)), F32)


X_TILE = 512
GRID_W = 64


def kernel(x, c, ctx, c_ctx, e_norm, e_w_mod, e_b_mod, e_w_in, e_a_conv_w, e_a_conv_b, e_a_ln_g, e_a_ln_b,
           e_a_pw_w, e_a_pw_b, e_b_fw, e_w_out, o_norm, o_w_mod, o_b_mod, o_w_in, o_c_conv_w, o_sink, o_w_out,
           final_norm):
    b, l, _ = x.shape
    lc = ctx.shape[1]
    assert x.shape == (8, 4096, D_MODEL) and ctx.shape == (8, 256, D_MODEL)
    depth = e_norm.shape[0] + o_norm.shape[0]
    assert depth % 2 == 0

    cv = jnp.zeros((MOD_ROWS, D_MODEL), F32).at[:b].set(c).at[CTX_ROW].set(c_ctx)
    mod_e = _modulation(cv, e_w_mod, e_b_mod)
    mod_o = _modulation(cv, o_w_mod, o_b_mod)
    gmat = _fourier_channel_mats(e_b_fw)

    cos_x, sin_x = _rope_tables(l, GRID_W)
    cos_c, sin_c = jnp.ones((lc, 128), F32), jnp.zeros((lc, 128), F32)

    row = lambda a: a.reshape(a.shape[0], 1, a.shape[-1])
    rep = lambda a: jnp.broadcast_to(a[..., None, :], a.shape[:-1] + (SUBLANES, HALF))
    front = (row(e_norm), e_w_in.astype(BF16), rep(e_a_conv_w), rep(e_a_conv_b), row(e_a_ln_g), row(e_a_ln_b),
             e_a_pw_w.astype(BF16), row(e_a_pw_b), gmat)
    we_out, wo_in, wo_out = e_w_out.astype(BF16), o_w_in.astype(BF16), o_w_out.astype(BF16)
    sink = o_sink
    norm_o = row(o_norm)
    fin = final_norm.reshape(1, D_MODEL)

    xc = ctx
    for i in range(depth // 2):
        last = i == depth // 2 - 1
        ya, y, sgb = _even_front(xc, mod_e, *front, layer=i, mod_row=CTX_ROW, tl=lc)
        xc, cbg_c, z_c, q_c, k_c, v_c, sag_c = _even_out_odd_in(
            y, sgb, ya, xc, mod_e, we_out, mod_o, norm_o, wo_in, cos_c, sin_c, layer=i, mod_row=CTX_ROW, tl=lc)
        ya, y, sgb = _even_front(x, mod_e, *front, layer=i, mod_row=None, tl=X_TILE)
        x, cbg, z, q, k, v, sag = _even_out_odd_in(
            y, sgb, ya, x, mod_e, we_out, mod_o, norm_o, wo_in, cos_x, sin_x, layer=i, mod_row=None, tl=X_TILE)
        x = _odd_out(x, mod_o, q, k, v, k_c, v_c, sink, z, cbg, sag, o_c_conv_w, wo_out, fin if last else None,
                     layer=i, mod_row=None, tq=X_TILE, window=True)
        if not last:
            xc = _odd_out(xc, mod_o, q_c, None, None, k_c, v_c, sink, z_c, cbg_c, sag_c, o_c_conv_w, wo_out, None,
                          layer=i, mod_row=CTX_ROW, tq=lc, window=False)
    return x
```

```python
import functools

import numpy as np
import jax
import jax.numpy as jnp
from jax import lax
from jax.experimental import pallas as pl
from jax.experimental.pallas import tpu as pltpu

D_MODEL = 1024
HALF = 512
A_CONV_W = 31
A_HALO = 16
C_HALO = 8
SUBLANES = 8
B_GROUPS = 4
B_GROUP_DIM = 128
HEAD_DIM = 64
N_Q_HEADS = 8
KV_WIDTH = 128
BLOCK = 128
ODD_IN = 3328
EVEN_IN = 2560
ROPE_BASE = 10000.0
NEG_INF = -1e30
LOG2E = 1.4426950408889634
Q_SCALE = HEAD_DIM ** -0.5 * LOG2E
EPS = 1e-6
RADIX = 16
MOD_ROWS = 16
CTX_ROW = 8
LANES = 128
MXU_COLS = 256
MXU_ROWS = 16
FFT_COLS = 2 * MXU_COLS
PM_SPLIT = ((), (0,), (1, 2), (3,))
PM_TAIL = (4, 5)
PM_ORDER = tuple(i for grp in PM_SPLIT for i in grp) + PM_TAIL
F32 = jnp.float32
BF16 = jnp.bfloat16
VMEM_LIMIT_BYTES = 48 * 1024 * 1024


def _params(*sem):
    return pltpu.CompilerParams(dimension_semantics=sem, vmem_limit_bytes=VMEM_LIMIT_BYTES)


def _dot(a, b):
    return jnp.dot(a, b, preferred_element_type=F32)


def _dot_nt(a, b):
    return lax.dot_general(a, b, (((1,), (1,)), ((), ())), preferred_element_type=F32)


def _pre_norm(x, g_ref, sh_ref, sc_ref):
    mult = g_ref[...] * (1.0 + sc_ref[...])
    ms = jnp.mean(x * x, axis=-1, keepdims=True)
    return (x * lax.rsqrt(ms + EPS)) * mult + sh_ref[...]


def _layer_spec(layer, shape):
    return pl.BlockSpec((None,) + tuple(shape), lambda bi, t: (layer,) + (0,) * len(shape))


def _mod_kernel(c_ref, w_ref, b_ref, o_ref):
    o_ref[0, :, 0, :] = _dot(jax.nn.silu(c_ref[...]), w_ref[0]) + b_ref[0]


def _modulation(cv, w_mod, b_mod):
    n = w_mod.shape[0]
    return pl.pallas_call(
        _mod_kernel,
        grid=(n, 3),
        in_specs=[pl.BlockSpec((MOD_ROWS, D_MODEL), lambda l, j: (0, 0)),
                  pl.BlockSpec((1, D_MODEL, D_MODEL), lambda l, j: (l, 0, j)),
                  pl.BlockSpec((1, 1, D_MODEL), lambda l, j: (l, 0, j))],
        out_specs=pl.BlockSpec((1, MOD_ROWS, 1, D_MODEL), lambda l, j: (l, 0, 0, j)),
        out_shape=jax.ShapeDtypeStruct((n, MOD_ROWS, 1, 3 * D_MODEL), F32),
        compiler_params=_params("arbitrary", "arbitrary"),
        name="modulation",
    )(cv, w_mod, b_mod.reshape(n, 1, 3 * D_MODEL))


def _mod_specs(layer, row):
    def spec(j):
        if row is None:
            return pl.BlockSpec((None, None, 1, D_MODEL), lambda b, t: (layer, b, 0, j))
        return pl.BlockSpec((None, None, 1, D_MODEL), lambda b, t: (layer, row, 0, j))
    return spec(0), spec(1), spec(2)


def _gprep_kernel(cd_ref, sd_ref, fw_ref, o_ref):
    o_ref[...] = jnp.zeros(o_ref.shape, o_ref.dtype)
    for g in range(B_GROUPS):
        fw = fw_ref[0, g]
        gc = jnp.dot(cd_ref[...], fw, preferred_element_type=F32, precision=lax.Precision.HIGHEST)
        gs = jnp.dot(sd_ref[...], fw, preferred_element_type=F32, precision=lax.Precision.HIGHEST)
        lo, hi = g * B_GROUP_DIM, (g + 1) * B_GROUP_DIM
        o_ref[0, lo:hi, lo:hi] = gc.astype(BF16)
        o_ref[0, lo:hi, HALF + lo:HALF + hi] = (-gs).astype(BF16)


def _fourier_channel_mats(fw):
    n = fw.shape[0]
    d = np.arange(B_GROUP_DIM)
    ang = 2.0 * np.pi * ((d[:, None] * d[None, :]) % B_GROUP_DIM) / B_GROUP_DIM
    cd = jnp.asarray(np.cos(ang) / np.sqrt(B_GROUP_DIM), F32)
    sd = jnp.asarray(np.sin(ang) / np.sqrt(B_GROUP_DIM), F32)
    mat = pl.BlockSpec((B_GROUP_DIM, B_GROUP_DIM), lambda i: (0, 0))
    return pl.pallas_call(
        _gprep_kernel,
        grid=(n,),
        in_specs=[mat, mat,
                  pl.BlockSpec((1, B_GROUPS, B_GROUP_DIM, B_GROUP_DIM), lambda i: (i, 0, 0, 0))],
        out_specs=pl.BlockSpec((1, HALF, 2 * HALF), lambda i: (i, 0, 0)),
        out_shape=jax.ShapeDtypeStruct((n, HALF, 2 * HALF), BF16),
        compiler_params=_params("arbitrary"),
        name="fourier_channel_mats",
    )(cd, sd, fw)


def _real_stack(m, real_only=False):
    top = np.concatenate([m.real, -m.imag], axis=1)
    if real_only:
        return top
    return np.concatenate([top, np.concatenate([m.imag, m.real], axis=1)], axis=0)


def _fft_tables_4096():
    r = np.arange(RADIX)
    eye = np.eye(RADIX)
    w16 = np.exp(-2j * np.pi * np.outer(r, r) / 16)
    w256 = np.exp(-2j * np.pi * np.outer(r, r) / 256)
    w4096 = np.exp(-2j * np.pi * np.outer(r, r) / 4096)
    ma = np.einsum("ka,kc,cd->kcad", w16, w4096, eye).reshape(256, 256) / 64.0
    mb = np.einsum("kb,kc,cd->kcbd", w16, w256, eye).reshape(256, 256)
    mc = np.einsum("kc,ae->kaec", w16, eye).reshape(256, 256)
    ph = 2.0 * np.pi * np.outer(r, r) / 256
    as_bf16 = lambda m: jnp.asarray(m, F32).astype(BF16)
    return (as_bf16(_real_stack(ma)), as_bf16(_real_stack(mb)), as_bf16(_real_stack(mc, real_only=True)),
            jnp.asarray(np.cos(ph), F32), jnp.asarray(np.sin(ph), F32))


def _fft_4096_kernel(z_ref, ma_ref, mb_ref, mc_ref, tc_ref, ts_ref, o_ref, s_scr):
    ct = z_ref.shape[-1]
    blk = RADIX * RADIX
    for b in range(RADIX):
        a1 = _dot(ma_ref[...], z_ref[0, :, :, b].reshape(2 * blk, ct))
        ar, ai = a1[:blk], a1[blk:]
        out_r, out_i = [], []
        for ka in range(RADIX):
            rr, ii = ar[ka * RADIX:(ka + 1) * RADIX], ai[ka * RADIX:(ka + 1) * RADIX]
            if b * ka == 0:
                out_r.append(rr)
                out_i.append(ii)
            else:
                c, s = tc_ref[b, ka], ts_ref[b, ka]
                out_r.append(rr * c + ii * s)
                out_i.append(ii * c - rr * s)
        s_scr[0, :, b] = jnp.concatenate(out_r, axis=0).astype(BF16).reshape(RADIX, RADIX, ct)
        s_scr[1, :, b] = jnp.concatenate(out_i, axis=0).astype(BF16).reshape(RADIX, RADIX, ct)
    for ka in range(RADIX):
        a2 = _dot(mb_ref[...], s_scr[:, ka].reshape(2 * blk, ct))
        s_scr[:, ka] = a2.astype(BF16).reshape(2, RADIX, RADIX, ct)
    for kb in range(RADIX):
        y = _dot(mc_ref[...], s_scr[:, :, kb].reshape(2 * blk, ct))
        o_ref[0, :, kb] = y.reshape(RADIX, RADIX, ct)


def _fft_4096(z):
    b = z.shape[0]
    ma, mb, mc, tc, ts = _fft_tables_4096()
    blk = RADIX * RADIX
    mat = lambda rows: pl.BlockSpec((rows, 2 * blk), lambda bi, j: (0, 0))
    smem = pl.BlockSpec(memory_space=pltpu.SMEM)
    y = pl.pallas_call(
        _fft_4096_kernel,
        grid=(b, HALF // FFT_COLS),
        in_specs=[pl.BlockSpec((1, 2, RADIX, RADIX, RADIX, FFT_COLS), lambda bi, j: (bi, 0, 0, 0, 0, j)),
                  mat(2 * blk), mat(2 * blk), mat(blk), smem, smem],
        out_specs=pl.BlockSpec((1, RADIX, RADIX, RADIX, FFT_COLS), lambda bi, j: (bi, 0, 0, 0, j)),
        out_shape=jax.ShapeDtypeStruct((b, RADIX, RADIX, RADIX, HALF), F32),
        scratch_shapes=[pltpu.VMEM((2, RADIX, RADIX, RADIX, FFT_COLS), BF16)],
        compiler_params=_params("arbitrary", "arbitrary"),
        name="fft_4096",
    )(z.reshape(b, 2, RADIX, RADIX, RADIX, HALF), ma, mb, mc, tc, ts)
    return y.reshape(b, RADIX ** 3, HALF)


def _fft_dense_kernel(z_ref, f_ref, o_ref):
    o_ref[0] = _dot(f_ref[...], z_ref[0].reshape(f_ref.shape[1], z_ref.shape[-1]))


def _fft_dense(z):
    b, _, l, _ = z.shape
    k = np.arange(l)
    ang = 2.0 * np.pi * ((k[:, None] * k[None, :]) % l) / l
    f = jnp.asarray(np.concatenate([np.cos(ang), np.sin(ang)], axis=1) / np.sqrt(l), F32).astype(BF16)
    return pl.pallas_call(
        _fft_dense_kernel,
        grid=(b, HALF // FFT_COLS),
        in_specs=[pl.BlockSpec((1, 2, l, FFT_COLS), lambda bi, j: (bi, 0, 0, j)),
                  pl.BlockSpec((l, 2 * l), lambda bi, j: (0, 0))],
        out_specs=pl.BlockSpec((1, l, FFT_COLS), lambda bi, j: (bi, 0, j)),
        out_shape=jax.ShapeDtypeStruct((b, l, HALF), F32),
        compiler_params=_params("arbitrary", "arbitrary"),
        name="fft_dense",
    )(z, f)


def _zeros_once_computed(tiles, zero, rows):
    acc = None
    for tile in tiles:
        bits = lax.bitcast_convert_type(tile, jnp.int32)
        bits = bits.reshape(tile.shape[0] // SUBLANES, SUBLANES, tile.shape[1])
        for i in range(0, bits.shape[0], MXU_ROWS // SUBLANES):
            for c in range(0, tile.shape[1], MXU_COLS):
                part = bits[i, :, c:c + LANES]
                acc = part if acc is None else acc | part
    z = lax.bitcast_convert_type(acc & zero, F32)
    return jnp.broadcast_to(z[None], (rows // SUBLANES, SUBLANES, LANES)).reshape(rows, LANES)


def _even_in_kernel(zero_ref, xp_ref, xm_ref, xn_ref, sh_ref, sc_ref, g_ref, win_ref, cw_ref, cb_ref, lg_ref, lb_ref,
                    pw_ref, pb_ref, gm_ref, ya_ref, z_ref, sgb_ref, u_scr, c_scr, *, tl, nt):
    t = pl.program_id(1)
    ext = tl + 2 * A_HALO
    xe = jnp.concatenate([xp_ref[0], xm_ref[0], xn_ref[0]], axis=0)
    he = _pre_norm(xe, g_ref, sh_ref, sc_ref).astype(BF16)
    row = lax.broadcasted_iota(jnp.int32, (ext, 1), 0)
    inside = ((row >= A_HALO) | (t > 0)) & ((row < tl + A_HALO) | (t < nt - 1))
    sub = lax.broadcasted_iota(jnp.int32, (ext // SUBLANES, SUBLANES, LANES), 1)
    rc = 128
    off = A_HALO - (A_CONV_W - 1) // 2
    hm = he[A_HALO:A_HALO + tl]
    pm_tiles = []
    pad = jnp.zeros((ext, LANES), F32)
    for blk in range(HALF // LANES):
        lanes = slice(blk * LANES, (blk + 1) * LANES)
        w_blk = jnp.concatenate([win_ref[:, lanes], win_ref[:, HALF + blk * LANES:HALF + (blk + 1) * LANES]], axis=1)
        pa = _dot(he, w_blk)
        u = jnp.where(inside, pa[:, :LANES] * jax.nn.sigmoid(pa[:, LANES:]), pad)
        if PM_SPLIT[blk]:
            new = [_dot(hm, win_ref[:, 2 * HALF + i * MXU_COLS:2 * HALF + (i + 1) * MXU_COLS])
                   for i in PM_SPLIT[blk]]
            pm_tiles += new
            pad = _zeros_once_computed(new, zero_ref[0], ext)
        u_scr[0, :, lanes] = u
        u3 = u.reshape(ext // SUBLANES, SUBLANES, LANES)
        for r in range(1, SUBLANES):
            rot = pltpu.roll(u3, SUBLANES - r, 1)
            nxt = jnp.concatenate([rot[1:], rot[:1]], axis=0)
            u_scr[r, :, lanes] = jnp.where(sub < SUBLANES - r, rot, nxt).reshape(ext, LANES)
        for r0 in range(0, tl, rc):
            acc = jnp.broadcast_to(cb_ref[:, lanes][None], (rc // SUBLANES, SUBLANES, LANES))
            for j in range(A_CONV_W):
                q, r = divmod(off + j, SUBLANES)
                uj = u_scr[r, r0 + SUBLANES * q:r0 + SUBLANES * q + rc, lanes]
                acc = acc + uj.reshape(rc // SUBLANES, SUBLANES, LANES) * cw_ref[j, :, lanes][None]
            c_scr[r0:r0 + rc, lanes] = acc.reshape(rc, LANES)
    pm_tiles += [_dot(hm, win_ref[:, 2 * HALF + i * MXU_COLS:2 * HALF + (i + 1) * MXU_COLS]) for i in PM_TAIL]
    pm = jnp.concatenate([pm_tiles[PM_ORDER.index(i)] for i in range(len(PM_ORDER))], axis=1)
    pq = _dot(pm[:, HALF:2 * HALF].astype(BF16), gm_ref[...])
    tie = _zeros_once_computed(pm_tiles[-len(PM_TAIL):] + [pq], zero_ref[0], tl)
    cv = c_scr[...] + jnp.concatenate([tie] * (HALF // LANES), axis=1)
    mu = jnp.mean(cv, axis=-1, keepdims=True)
    cen = cv - mu
    var = jnp.mean(cen * cen, axis=-1, keepdims=True)
    ln = cen * lax.rsqrt(var + EPS) * lg_ref[...] + lb_ref[...]
    ya = (_dot(jax.nn.silu(ln).astype(BF16), pw_ref[...]) + pb_ref[...]) * jax.nn.silu(pm[:, :HALF])
    ya_ref[0] = ya.astype(BF16)
    z_ref[0, 0] = pq[:, :HALF].astype(BF16)
    z_ref[0, 1] = pq[:, HALF:].astype(BF16)
    sgb_ref[0] = jax.nn.silu(pm[:, 2 * HALF:])


def _even_front(x, mod, norm_g, w_in, conv_w, conv_b, ln_g, ln_b, pw_w, pw_b, gmat, *, layer, mod_row, tl):
    b, l, _ = x.shape
    nt = l // tl
    hb = tl // A_HALO
    sh, sc, _ = _mod_specs(layer, mod_row)
    const = functools.partial(_layer_spec, layer)
    tile = lambda w: pl.BlockSpec((1, tl, w), lambda bi, t: (bi, t, 0))
    ya, z, sgb = pl.pallas_call(
        functools.partial(_even_in_kernel, tl=tl, nt=nt),
        grid=(b, nt),
        in_specs=[pl.BlockSpec(memory_space=pltpu.SMEM),
                  pl.BlockSpec((1, A_HALO, D_MODEL), lambda bi, t: (bi, jnp.maximum(t * hb - 1, 0), 0)),
                  tile(D_MODEL),
                  pl.BlockSpec((1, A_HALO, D_MODEL), lambda bi, t: (bi, jnp.minimum((t + 1) * hb, l // A_HALO - 1), 0)),
                  sh, sc, const((1, D_MODEL)), const((D_MODEL, EVEN_IN)), const((A_CONV_W, SUBLANES, HALF)),
                  const((SUBLANES, HALF)), const((1, HALF)), const((1, HALF)), const((HALF, HALF)), const((1, HALF)),
                  const((HALF, 2 * HALF))],
        out_specs=[tile(HALF),
                   pl.BlockSpec((1, 2, tl, HALF), lambda bi, t: (bi, 0, t, 0)),
                   tile(HALF)],
        out_shape=[jax.ShapeDtypeStruct((b, l, HALF), BF16),
                   jax.ShapeDtypeStruct((b, 2, l, HALF), BF16),
                   jax.ShapeDtypeStruct((b, l, HALF), F32)],
        scratch_shapes=[pltpu.VMEM((SUBLANES, tl + 2 * A_HALO, HALF), F32), pltpu.VMEM((tl, HALF), F32)],
        compiler_params=_params("arbitrary", "arbitrary"),
        name="even_in",
    )(jnp.zeros((1,), jnp.int32), x, x, x, mod, mod, norm_g, w_in, conv_w, conv_b, ln_g, ln_b, pw_w, pw_b, gmat)

    y = _fft_4096(z) if l == RADIX ** 3 else _fft_dense(z)
    return ya, y, sgb


def _rope(t, cos_ref, sin_ref):
    reps = t.shape[1] // 128
    c = jnp.concatenate([cos_ref[...]] * reps, axis=1) if reps > 1 else cos_ref[...]
    s = jnp.concatenate([sin_ref[...]] * reps, axis=1) if reps > 1 else sin_ref[...]
    lane = lax.broadcasted_iota(jnp.int32, t.shape, 1)
    partner = jnp.where((lane & 16) == 0, pltpu.roll(t, t.shape[1] - 16, 1), pltpu.roll(t, 16, 1))
    return t * c + partner * s


def _pair_heads(t):
    n = N_Q_HEADS // 2
    vregs = [t[:, j * LANES:(j + 1) * LANES] for j in range(n)]
    swapped = [pltpu.roll(v, HEAD_DIM, 1) for v in vregs]
    low = lax.broadcasted_iota(jnp.int32, vregs[0].shape, 1) < HEAD_DIM
    out = []
    for g in range(n):
        a, b = g // 2, (g + n) // 2
        out.append(jnp.where(low, vregs[a], swapped[b]) if g % 2 == 0 else jnp.where(low, swapped[a], vregs[b]))
    return jnp.concatenate(out, axis=1)


def _even_out_odd_in_kernel(y_ref, sgb_ref, ya_ref, x_ref, gt_ref, wout_ref, sh_ref, sc_ref, g_ref, win_ref,
                            cos_ref, sin_ref, xo_ref, cbg_ref, z_ref, q_ref, k_ref, v_ref, sag_ref):
    yb = (y_ref[0] * sgb_ref[0]).astype(BF16)
    mix = _dot(jnp.concatenate([ya_ref[0], yb], axis=1), wout_ref[...])
    x = x_ref[0] + gt_ref[...] * mix
    xo_ref[0] = x
    h = _pre_norm(x, g_ref, sh_ref, sc_ref).astype(BF16)
    q0 = 4 * HALF
    pa = _dot(h, win_ref[:, q0:ODD_IN])
    q = _rope(pa[:, 0:HALF], cos_ref, sin_ref) * Q_SCALE
    q_ref[0] = _pair_heads(q).astype(BF16)
    k_ref[0] = _rope(pa[:, HALF:HALF + KV_WIDTH], cos_ref, sin_ref).astype(BF16)
    v_ref[0] = pa[:, HALF + KV_WIDTH:HALF + 2 * KV_WIDTH].astype(BF16)
    sag_ref[0] = _pair_heads(jax.nn.silu(pa[:, HALF + 2 * KV_WIDTH:]))
    pc = _dot(h, win_ref[:, 0:q0])
    cbg_ref[0] = pc[:, 0:HALF] * jax.nn.silu(pc[:, 3 * HALF:4 * HALF])
    z_ref[0] = pc[:, HALF:2 * HALF] * pc[:, 2 * HALF:3 * HALF]


def _even_out_odd_in(y, sgb, ya, x, mod_e, w_out, mod_o, norm_g, w_in, cos_t, sin_t, *, layer, mod_row, tl):
    b, l, _ = x.shape
    _, _, gt = _mod_specs(layer, mod_row)
    sh, sc, _ = _mod_specs(layer, mod_row)
    tile = lambda w: pl.BlockSpec((1, tl, w), lambda bi, t: (bi, t, 0))
    const = functools.partial(_layer_spec, layer)
    return pl.pallas_call(
        _even_out_odd_in_kernel,
        grid=(b, l // tl),
        in_specs=[tile(HALF), tile(HALF), tile(HALF), tile(D_MODEL), gt, const((D_MODEL, D_MODEL)),
                  sh, sc, const((1, D_MODEL)), const((D_MODEL, ODD_IN)),
                  pl.BlockSpec((tl, 128), lambda bi, t: (t, 0)),
                  pl.BlockSpec((tl, 128), lambda bi, t: (t, 0))],
        out_specs=[tile(D_MODEL), tile(HALF), tile(HALF), tile(HALF), tile(KV_WIDTH), tile(KV_WIDTH), tile(HALF)],
        out_shape=[jax.ShapeDtypeStruct((b, l, D_MODEL), F32),
                   jax.ShapeDtypeStruct((b, l, HALF), F32), jax.ShapeDtypeStruct((b, l, HALF), F32),
                   jax.ShapeDtypeStruct((b, l, HALF), BF16), jax.ShapeDtypeStruct((b, l, KV_WIDTH), BF16),
                   jax.ShapeDtypeStruct((b, l, KV_WIDTH), BF16), jax.ShapeDtypeStruct((b, l, HALF), F32)],
        compiler_params=_params("arbitrary", "arbitrary"),
        name="even_out_odd_in",
    )(y, sgb, ya, x, mod_e, w_out, mod_o, mod_o, norm_g, w_in, cos_t, sin_t)


def _odd_out_kernel(*refs, layer, tq, nt, window, final):
    refs = list(refs)
    q_ref = refs.pop(0)
    if window:
        kp_ref, km_ref, kn_ref, vp_ref, vm_ref, vn_ref = refs[:6]
        refs = refs[6:]
    kc_ref, vc_ref, sink_ref, zp_ref, zm_ref, zn_ref, cbg_ref, sag_ref, x_ref, gt_ref, cw_ref, wout_ref = refs[:12]
    refs = refs[12:]
    fn_ref = refs.pop(0) if final else None
    o_ref = refs.pop(0)

    t = pl.program_id(1)
    nqb = tq // BLOCK
    lane = lax.broadcasted_iota(jnp.int32, (BLOCK, 128), 1)
    low = lane < HEAD_DIM
    kc, vc = kc_ref[0], vc_ref[0]
    if window:
        kext = jnp.concatenate([kp_ref[0], km_ref[0], kn_ref[0]], axis=0)
        vext = jnp.concatenate([vp_ref[0], vm_ref[0], vn_ref[0]], axis=0)
        qi = lax.broadcasted_iota(jnp.int32, (BLOCK, 3 * BLOCK), 0)
        kj = lax.broadcasted_iota(jnp.int32, (BLOCK, 3 * BLOCK), 1)
        band = (kj - qi >= 0) & (kj - qi <= 2 * BLOCK)

    att_rows = []
    zero = jnp.zeros((BLOCK, LANES), BF16)
    heads_per_kv = N_Q_HEADS // 2
    for qb in range(nqb):
        qg = q_ref[0, qb * BLOCK:(qb + 1) * BLOCK, :]
        slots = []
        for g in range(heads_per_kv):
            qq = qg[:, g * LANES:(g + 1) * LANES]
            slots += [jnp.where(low, qq, zero), jnp.where(low, zero, qq)]
        qs = jnp.concatenate(slots, axis=0)
        s_c = _dot_nt(qs, kc)
        if window:
            s_w = _dot_nt(qs, kext[qb * BLOCK:(qb + 3) * BLOCK])
            valid = band
            if qb == 0:
                valid = valid & ((kj >= BLOCK) | (t > 0))
            if qb == nqb - 1:
                valid = valid & ((kj < 2 * BLOCK) | (t < nt - 1))
        pws, pcs, dens = [], [], []
        for s in range(N_Q_HEADS):
            head = s // 2 + heads_per_kv * (s % 2)
            sink = sink_ref[layer, head] * LOG2E
            sc_s = s_c[s * BLOCK:(s + 1) * BLOCK]
            m = jnp.maximum(jnp.max(sc_s, axis=-1, keepdims=True), sink)
            if window:
                sw_s = jnp.where(valid, s_w[s * BLOCK:(s + 1) * BLOCK], NEG_INF)
                m = jnp.maximum(m, jnp.max(sw_s, axis=-1, keepdims=True))
            pc = jnp.exp2(sc_s - m)
            den = jnp.exp2(sink - m) + jnp.sum(pc, axis=-1, keepdims=True)
            if window:
                pw = jnp.exp2(sw_s - m)
                den = den + jnp.sum(pw, axis=-1, keepdims=True)
                pws.append(pw.astype(BF16))
            pcs.append(pc.astype(BF16))
            dens.append(den)
        o = _dot(jnp.concatenate(pcs, axis=0), vc)
        if window:
            o = o + _dot(jnp.concatenate(pws, axis=0), vext[qb * BLOCK:(qb + 3) * BLOCK])
        o = o / jnp.concatenate(dens, axis=0)
        att_rows.append(jnp.concatenate(
            [jnp.where(low, o[(2 * g) * BLOCK:(2 * g + 1) * BLOCK], o[(2 * g + 1) * BLOCK:(2 * g + 2) * BLOCK])
             for g in range(heads_per_kv)], axis=1))
    att = jnp.concatenate(att_rows, axis=0) if nqb > 1 else att_rows[0]
    yd = att * sag_ref[0]

    zp = jnp.where(t > 0, zp_ref[0], 0.0)
    zn = jnp.where(t < nt - 1, zn_ref[0], 0.0)
    zext = jnp.concatenate([zp, zm_ref[0], zn], axis=0)
    conv = (zext[C_HALO - 1:C_HALO - 1 + tq] * cw_ref[0:1, :] + zext[C_HALO:C_HALO + tq] * cw_ref[1:2, :]
            + zext[C_HALO + 1:C_HALO + 1 + tq] * cw_ref[2:3, :])
    yc = cbg_ref[0] * conv
    order = [g + (N_Q_HEADS // 2) * half for g in range(N_Q_HEADS // 2) for half in range(2)]
    w_att = jnp.concatenate([wout_ref[HALF + h * HEAD_DIM:HALF + (h + 1) * HEAD_DIM] for h in order], axis=0)
    mix = _dot(yc.astype(BF16), wout_ref[0:HALF]) + _dot(yd.astype(BF16), w_att)
    out = x_ref[0] + gt_ref[...] * mix
    if final:
        ms = jnp.mean(out * out, axis=-1, keepdims=True)
        out = out * lax.rsqrt(ms + EPS) * fn_ref[...]
    o_ref[0] = out


def _odd_out(x, mod, q, k, v, kc, vc, sink, z, cbg, sag, conv_w, w_out, final_norm, *, layer, mod_row, tq, window):
    b, l, _ = x.shape
    nt = l // tq
    nqb = tq // BLOCK
    lc = kc.shape[1]
    _, _, gt = _mod_specs(layer, mod_row)
    tile = lambda w: pl.BlockSpec((1, tq, w), lambda bi, t: (bi, t, 0))
    prev_blk = pl.BlockSpec((1, BLOCK, KV_WIDTH), lambda bi, t: (bi, jnp.maximum(t * nqb - 1, 0), 0))
    next_blk = pl.BlockSpec((1, BLOCK, KV_WIDTH), lambda bi, t: (bi, jnp.minimum((t + 1) * nqb, l // BLOCK - 1), 0))
    zh = tq // C_HALO
    args, specs = [q], [tile(HALF)]
    if window:
        args += [k, k, k, v, v, v]
        specs += [prev_blk, tile(KV_WIDTH), next_blk, prev_blk, tile(KV_WIDTH), next_blk]
    args += [kc, vc, sink, z, z, z, cbg, sag, x, mod, conv_w, w_out]
    specs += [pl.BlockSpec((1, lc, KV_WIDTH), lambda bi, t: (bi, 0, 0)),
              pl.BlockSpec((1, lc, KV_WIDTH), lambda bi, t: (bi, 0, 0)),
              pl.BlockSpec(memory_space=pltpu.SMEM),
              pl.BlockSpec((1, C_HALO, HALF), lambda bi, t: (bi, jnp.maximum(t * zh - 1, 0), 0)),
              tile(HALF),
              pl.BlockSpec((1, C_HALO, HALF), lambda bi, t: (bi, jnp.minimum((t + 1) * zh, l // C_HALO - 1), 0)),
              tile(HALF), tile(HALF), tile(D_MODEL), gt,
              _layer_spec(layer, (3, HALF)), _layer_spec(layer, (D_MODEL, D_MODEL))]
    if final_norm is not None:
        args.append(final_norm)
        specs.append(pl.BlockSpec((1, D_MODEL), lambda bi, t: (0, 0)))
    return pl.pallas_call(
        functools.partial(_odd_out_kernel, layer=layer, tq=tq, nt=nt, window=window, final=final_norm is not None),
        grid=(b, nt),
        in_specs=specs,
        out_specs=tile(D_MODEL),
        out_shape=jax.ShapeDtypeStruct((b, l, D_MODEL), F32),
        compiler_params=_params("arbitrary", "arbitrary"),
        name="odd_out_window" if window else "odd_out_context",
    )(*args)


def _rope_tables(l, grid_w):
    nf = HEAD_DIM // 4
    pos = np.arange(l)
    inv = np.power(ROPE_BASE, -np.arange(nf, dtype=np.float64) / nf)
    ang = np.stack([(pos // grid_w)[:, None] * inv, (pos % grid_w)[:, None] * inv], axis=1)
    cos, sin = np.cos(ang), np.sin(ang)
    cos_t = np.stack([cos, cos], axis=2).reshape(l, HEAD_DIM)
    sin_t = np.stack([-sin, sin], axis=2).reshape(l, HEAD_DIM)
    return jnp.asarray(np.tile(cos_t, (1, 2)), F32), jnp.asarray(np.tile(sin_t, (1, 2)), F32)


X_TILE = 512
GRID_W = 64


def kernel(x, c, ctx, c_ctx, e_norm, e_w_mod, e_b_mod, e_w_in, e_a_conv_w, e_a_conv_b, e_a_ln_g, e_a_ln_b,
           e_a_pw_w, e_a_pw_b, e_b_fw, e_w_out, o_norm, o_w_mod, o_b_mod, o_w_in, o_c_conv_w, o_sink, o_w_out,
           final_norm):
    b, l, _ = x.shape
    lc = ctx.shape[1]
    assert x.shape == (8, 4096, D_MODEL) and ctx.shape == (8, 256, D_MODEL)
    depth = e_norm.shape[0] + o_norm.shape[0]
    assert depth % 2 == 0

    cv = jnp.zeros((MOD_ROWS, D_MODEL), F32).at[:b].set(c).at[CTX_ROW].set(c_ctx)
    mod_e = _modulation(cv, e_w_mod, e_b_mod)
    mod_o = _modulation(cv, o_w_mod, o_b_mod)
    gmat = _fourier_channel_mats(e_b_fw)

    cos_x, sin_x = _rope_tables(l, GRID_W)
    cos_c, sin_c = jnp.ones((lc, 128), F32), jnp.zeros((lc, 128), F32)

    row = lambda a: a.reshape(a.shape[0], 1, a.shape[-1])
    rep = lambda a: jnp.broadcast_to(a[..., None, :], a.shape[:-1] + (SUBLANES, HALF))
    front = (row(e_norm), e_w_in.astype(BF16), rep(e_a_conv_w), rep(e_a_conv_b), row(e_a_ln_g), row(e_a_ln_b),
             e_a_pw_w.astype(BF16), row(e_a_pw_b), gmat)
    we_out, wo_in, wo_out = e_w_out.astype(BF16), o_w_in.astype(BF16), o_w_out.astype(BF16)
    sink = o_sink
    norm_o = row(o_norm)
    fin = final_norm.reshape(1, D_MODEL)

    xc = ctx
    for i in range(depth // 2):
        last = i == depth // 2 - 1
        ya, y, sgb = _even_front(xc, mod_e, *front, layer=i, mod_row=CTX_ROW, tl=lc)
        xc, cbg_c, z_c, q_c, k_c, v_c, sag_c = _even_out_odd_in(
            y, sgb, ya, xc, mod_e, we_out, mod_o, norm_o, wo_in, cos_c, sin_c, layer=i, mod_row=CTX_ROW, tl=lc)
        ya, y, sgb = _even_front(x, mod_e, *front, layer=i, mod_row=None, tl=X_TILE)
        x, cbg, z, q, k, v, sag = _even_out_odd_in(
            y, sgb, ya, x, mod_e, we_out, mod_o, norm_o, wo_in, cos_x, sin_x, layer=i, mod_row=None, tl=X_TILE)
        x = _odd_out(x, mod_o, q, k, v, k_c, v_c, sink, z, cbg, sag, o_c_conv_w, wo_out, fin if last else None,
                     layer=i, mod_row=None, tq=2 * X_TILE, window=True)
        if not last:
            xc = _odd_out(xc, mod_o, q_c, None, None, k_c, v_c, sink, z_c, cbg_c, sag_c, o_c_conv_w, wo_out, None,
                          layer=i, mod_row=CTX_ROW, tq=lc, window=False)
    return x
```

```python
import functools

import numpy as np
import jax
import jax.numpy as jnp
from jax import lax
from jax.experimental import pallas as pl
from jax.experimental.pallas import tpu as pltpu

D_MODEL = 1024
HALF = 512
A_CONV_W = 31
A_HALO = 16
C_HALO = 8
SUBLANES = 8
B_GROUPS = 4
B_GROUP_DIM = 128
HEAD_DIM = 64
N_Q_HEADS = 8
KV_WIDTH = 128
BLOCK = 128
ODD_IN = 3328
EVEN_IN = 2560
ROPE_BASE = 10000.0
NEG_INF = -1e30
LOG2E = 1.4426950408889634
Q_SCALE = HEAD_DIM ** -0.5 * LOG2E
EPS = 1e-6
RADIX = 16
MOD_ROWS = 16
CTX_ROW = 8
LANES = 128
MXU_COLS = 256
MXU_ROWS = 16
FFT_COLS = MXU_COLS
PM_SPLIT = ((), (0,), (1, 2), (3,))
PM_TAIL = (4, 5)
PM_ORDER = tuple(i for grp in PM_SPLIT for i in grp) + PM_TAIL
F32 = jnp.float32
BF16 = jnp.bfloat16
VMEM_LIMIT_BYTES = 48 * 1024 * 1024


def _params(*sem):
    return pltpu.CompilerParams(dimension_semantics=sem, vmem_limit_bytes=VMEM_LIMIT_BYTES)


def _dot(a, b):
    return jnp.dot(a, b, preferred_element_type=F32)


def _dot_nt(a, b):
    return lax.dot_general(a, b, (((1,), (1,)), ((), ())), preferred_element_type=F32)


def _pre_norm(x, g_ref, sh_ref, sc_ref):
    mult = g_ref[...] * (1.0 + sc_ref[...])
    ms = jnp.mean(x * x, axis=-1, keepdims=True)
    return (x * lax.rsqrt(ms + EPS)) * mult + sh_ref[...]


def _layer_spec(layer, shape):
    return pl.BlockSpec((None,) + tuple(shape), lambda bi, t: (layer,) + (0,) * len(shape))


def _mod_kernel(c_ref, w_ref, b_ref, o_ref):
    o_ref[0, :, 0, :] = _dot(jax.nn.silu(c_ref[...]), w_ref[0]) + b_ref[0]


def _modulation(cv, w_mod, b_mod):
    n = w_mod.shape[0]
    return pl.pallas_call(
        _mod_kernel,
        grid=(n, 3),
        in_specs=[pl.BlockSpec((MOD_ROWS, D_MODEL), lambda l, j: (0, 0)),
                  pl.BlockSpec((1, D_MODEL, D_MODEL), lambda l, j: (l, 0, j)),
                  pl.BlockSpec((1, 1, D_MODEL), lambda l, j: (l, 0, j))],
        out_specs=pl.BlockSpec((1, MOD_ROWS, 1, D_MODEL), lambda l, j: (l, 0, 0, j)),
        out_shape=jax.ShapeDtypeStruct((n, MOD_ROWS, 1, 3 * D_MODEL), F32),
        compiler_params=_params("arbitrary", "arbitrary"),
        name="modulation",
    )(cv, w_mod, b_mod.reshape(n, 1, 3 * D_MODEL))


def _mod_specs(layer, row):
    def spec(j):
        if row is None:
            return pl.BlockSpec((None, None, 1, D_MODEL), lambda b, t: (layer, b, 0, j))
        return pl.BlockSpec((None, None, 1, D_MODEL), lambda b, t: (layer, row, 0, j))
    return spec(0), spec(1), spec(2)


def _gprep_kernel(cd_ref, sd_ref, fw_ref, o_ref):
    o_ref[...] = jnp.zeros(o_ref.shape, o_ref.dtype)
    for g in range(B_GROUPS):
        fw = fw_ref[0, g]
        gc = jnp.dot(cd_ref[...], fw, preferred_element_type=F32, precision=lax.Precision.HIGHEST)
        gs = jnp.dot(sd_ref[...], fw, preferred_element_type=F32, precision=lax.Precision.HIGHEST)
        lo, hi = g * B_GROUP_DIM, (g + 1) * B_GROUP_DIM
        o_ref[0, lo:hi, lo:hi] = gc.astype(BF16)
        o_ref[0, lo:hi, HALF + lo:HALF + hi] = (-gs).astype(BF16)


def _fourier_channel_mats(fw):
    n = fw.shape[0]
    d = np.arange(B_GROUP_DIM)
    ang = 2.0 * np.pi * ((d[:, None] * d[None, :]) % B_GROUP_DIM) / B_GROUP_DIM
    cd = jnp.asarray(np.cos(ang) / np.sqrt(B_GROUP_DIM), F32)
    sd = jnp.asarray(np.sin(ang) / np.sqrt(B_GROUP_DIM), F32)
    mat = pl.BlockSpec((B_GROUP_DIM, B_GROUP_DIM), lambda i: (0, 0))
    return pl.pallas_call(
        _gprep_kernel,
        grid=(n,),
        in_specs=[mat, mat,
                  pl.BlockSpec((1, B_GROUPS, B_GROUP_DIM, B_GROUP_DIM), lambda i: (i, 0, 0, 0))],
        out_specs=pl.BlockSpec((1, HALF, 2 * HALF), lambda i: (i, 0, 0)),
        out_shape=jax.ShapeDtypeStruct((n, HALF, 2 * HALF), BF16),
        compiler_params=_params("arbitrary"),
        name="fourier_channel_mats",
    )(cd, sd, fw)


def _real_stack(m, real_only=False):
    top = np.concatenate([m.real, -m.imag], axis=1)
    if real_only:
        return top
    return np.concatenate([top, np.concatenate([m.imag, m.real], axis=1)], axis=0)


def _fft_tables_4096():
    r = np.arange(RADIX)
    eye = np.eye(RADIX)
    w16 = np.exp(-2j * np.pi * np.outer(r, r) / 16)
    w256 = np.exp(-2j * np.pi * np.outer(r, r) / 256)
    w4096 = np.exp(-2j * np.pi * np.outer(r, r) / 4096)
    ma = np.einsum("ka,kc,cd->kcad", w16, w4096, eye).reshape(256, 256) / 64.0
    mb = np.einsum("kb,kc,cd->kcbd", w16, w256, eye).reshape(256, 256)
    mc = np.einsum("kc,ae->kaec", w16, eye).reshape(256, 256)
    ph = 2.0 * np.pi * np.outer(r, r) / 256
    as_bf16 = lambda m: jnp.asarray(m, F32).astype(BF16)
    return (as_bf16(_real_stack(ma)), as_bf16(_real_stack(mb)), as_bf16(_real_stack(mc, real_only=True)),
            jnp.asarray(np.cos(ph), F32), jnp.asarray(np.sin(ph), F32))


def _fft_4096_kernel(z_ref, ma_ref, mb_ref, mc_ref, tc_ref, ts_ref, o_ref, s_scr):
    ct = z_ref.shape[-1]
    blk = RADIX * RADIX
    for b in range(RADIX):
        a1 = _dot(ma_ref[...], z_ref[0, :, :, b].reshape(2 * blk, ct))
        ar, ai = a1[:blk], a1[blk:]
        out_r, out_i = [], []
        for ka in range(RADIX):
            rr, ii = ar[ka * RADIX:(ka + 1) * RADIX], ai[ka * RADIX:(ka + 1) * RADIX]
            if b * ka == 0:
                out_r.append(rr)
                out_i.append(ii)
            else:
                c, s = tc_ref[b, ka], ts_ref[b, ka]
                out_r.append(rr * c + ii * s)
                out_i.append(ii * c - rr * s)
        s_scr[0, :, b] = jnp.concatenate(out_r, axis=0).astype(BF16).reshape(RADIX, RADIX, ct)
        s_scr[1, :, b] = jnp.concatenate(out_i, axis=0).astype(BF16).reshape(RADIX, RADIX, ct)
    for ka in range(RADIX):
        a2 = _dot(mb_ref[...], s_scr[:, ka].reshape(2 * blk, ct))
        s_scr[:, ka] = a2.astype(BF16).reshape(2, RADIX, RADIX, ct)
    for kb in range(RADIX):
        y = _dot(mc_ref[...], s_scr[:, :, kb].reshape(2 * blk, ct))
        o_ref[0, :, kb] = y.reshape(RADIX, RADIX, ct)


def _fft_4096(z):
    b = z.shape[0]
    ma, mb, mc, tc, ts = _fft_tables_4096()
    blk = RADIX * RADIX
    mat = lambda rows: pl.BlockSpec((rows, 2 * blk), lambda bi, j: (0, 0))
    smem = pl.BlockSpec(memory_space=pltpu.SMEM)
    y = pl.pallas_call(
        _fft_4096_kernel,
        grid=(b, HALF // FFT_COLS),
        in_specs=[pl.BlockSpec((1, 2, RADIX, RADIX, RADIX, FFT_COLS), lambda bi, j: (bi, 0, 0, 0, 0, j)),
                  mat(2 * blk), mat(2 * blk), mat(blk), smem, smem],
        out_specs=pl.BlockSpec((1, RADIX, RADIX, RADIX, FFT_COLS), lambda bi, j: (bi, 0, 0, 0, j)),
        out_shape=jax.ShapeDtypeStruct((b, RADIX, RADIX, RADIX, HALF), F32),
        scratch_shapes=[pltpu.VMEM((2, RADIX, RADIX, RADIX, FFT_COLS), BF16)],
        compiler_params=_params("arbitrary", "arbitrary"),
        name="fft_4096",
    )(z.reshape(b, 2, RADIX, RADIX, RADIX, HALF), ma, mb, mc, tc, ts)
    return y.reshape(b, RADIX ** 3, HALF)


def _zeros_once_computed(tiles, zero, rows):
    acc = None
    for tile in tiles:
        bits = lax.bitcast_convert_type(tile, jnp.int32)
        bits = bits.reshape(tile.shape[0] // SUBLANES, SUBLANES, tile.shape[1])
        for i in range(0, bits.shape[0], MXU_ROWS // SUBLANES):
            for c in range(0, tile.shape[1], MXU_COLS):
                part = bits[i, :, c:c + LANES]
                acc = part if acc is None else acc | part
    z = lax.bitcast_convert_type(acc & zero, F32)
    return jnp.broadcast_to(z[None], (rows // SUBLANES, SUBLANES, LANES)).reshape(rows, LANES)


def _even_in_body(xe, t, zero_ref, sh_ref, sc_ref, g_ref, win_ref, cw_ref, cb_ref, lg_ref, lb_ref, pw_ref, pb_ref,
                  gm_ref, u_scr, c_scr, *, tl, nt):
    ext = tl + 2 * A_HALO
    he = _pre_norm(xe, g_ref, sh_ref, sc_ref).astype(BF16)
    row = lax.broadcasted_iota(jnp.int32, (ext, 1), 0)
    inside = ((row >= A_HALO) | (t > 0)) & ((row < tl + A_HALO) | (t < nt - 1))
    sub = lax.broadcasted_iota(jnp.int32, (ext // SUBLANES, SUBLANES, LANES), 1)
    rc = 128
    off = A_HALO - (A_CONV_W - 1) // 2
    hm = he[A_HALO:A_HALO + tl]
    pm_tiles = []
    pad = jnp.zeros((ext, LANES), F32)
    for blk in range(HALF // LANES):
        lanes = slice(blk * LANES, (blk + 1) * LANES)
        w_blk = jnp.concatenate([win_ref[:, lanes], win_ref[:, HALF + blk * LANES:HALF + (blk + 1) * LANES]], axis=1)
        pa = _dot(he, w_blk)
        u = jnp.where(inside, pa[:, :LANES] * jax.nn.sigmoid(pa[:, LANES:]), pad)
        if PM_SPLIT[blk]:
            new = [_dot(hm, win_ref[:, 2 * HALF + i * MXU_COLS:2 * HALF + (i + 1) * MXU_COLS])
                   for i in PM_SPLIT[blk]]
            pm_tiles += new
            pad = _zeros_once_computed(new, zero_ref[0], ext)
        u_scr[0, :, lanes] = u
        u3 = u.reshape(ext // SUBLANES, SUBLANES, LANES)
        for r in range(1, SUBLANES):
            rot = pltpu.roll(u3, SUBLANES - r, 1)
            nxt = jnp.concatenate([rot[1:], rot[:1]], axis=0)
            u_scr[r, :, lanes] = jnp.where(sub < SUBLANES - r, rot, nxt).reshape(ext, LANES)
        for r0 in range(0, tl, rc):
            acc = jnp.broadcast_to(cb_ref[:, lanes][None], (rc // SUBLANES, SUBLANES, LANES))
            for j in range(A_CONV_W):
                q, r = divmod(off + j, SUBLANES)
                uj = u_scr[r, r0 + SUBLANES * q:r0 + SUBLANES * q + rc, lanes]
                acc = acc + uj.reshape(rc // SUBLANES, SUBLANES, LANES) * cw_ref[j, :, lanes][None]
            c_scr[r0:r0 + rc, lanes] = acc.reshape(rc, LANES)
    pm_tiles += [_dot(hm, win_ref[:, 2 * HALF + i * MXU_COLS:2 * HALF + (i + 1) * MXU_COLS]) for i in PM_TAIL]
    pm = jnp.concatenate([pm_tiles[PM_ORDER.index(i)] for i in range(len(PM_ORDER))], axis=1)
    pq = _dot(pm[:, HALF:2 * HALF].astype(BF16), gm_ref[...])
    tie = _zeros_once_computed(pm_tiles[-len(PM_TAIL):] + [pq], zero_ref[0], tl)
    cv = c_scr[...] + jnp.concatenate([tie] * (HALF // LANES), axis=1)
    mu = jnp.mean(cv, axis=-1, keepdims=True)
    cen = cv - mu
    var = jnp.mean(cen * cen, axis=-1, keepdims=True)
    ln = cen * lax.rsqrt(var + EPS) * lg_ref[...] + lb_ref[...]
    ya = (_dot(jax.nn.silu(ln).astype(BF16), pw_ref[...]) + pb_ref[...]) * jax.nn.silu(pm[:, :HALF])
    return ya.astype(BF16), pq, jax.nn.silu(pm[:, 2 * HALF:])


def _even_in_kernel(zero_ref, xp_ref, xm_ref, xn_ref, sh_ref, sc_ref, g_ref, win_ref, cw_ref, cb_ref, lg_ref, lb_ref,
                    pw_ref, pb_ref, gm_ref, ya_ref, z_ref, sgb_ref, u_scr, c_scr, *, tl, nt):
    xe = jnp.concatenate([xp_ref[0], xm_ref[0], xn_ref[0]], axis=0)
    ya, pq, sgb = _even_in_body(xe, pl.program_id(1), zero_ref, sh_ref, sc_ref, g_ref, win_ref, cw_ref, cb_ref,
                                lg_ref, lb_ref, pw_ref, pb_ref, gm_ref, u_scr, c_scr, tl=tl, nt=nt)
    ya_ref[0] = ya
    z_ref[0, 0] = pq[:, :HALF].astype(BF16)
    z_ref[0, 1] = pq[:, HALF:].astype(BF16)
    sgb_ref[0] = sgb


def _even_front(x, mod, norm_g, w_in, conv_w, conv_b, ln_g, ln_b, pw_w, pw_b, gmat, *, layer, mod_row, tl):
    b, l, _ = x.shape
    nt = l // tl
    hb = tl // A_HALO
    sh, sc, _ = _mod_specs(layer, mod_row)
    const = functools.partial(_layer_spec, layer)
    tile = lambda w: pl.BlockSpec((1, tl, w), lambda bi, t: (bi, t, 0))
    ya, z, sgb = pl.pallas_call(
        functools.partial(_even_in_kernel, tl=tl, nt=nt),
        grid=(b, nt),
        in_specs=[pl.BlockSpec(memory_space=pltpu.SMEM),
                  pl.BlockSpec((1, A_HALO, D_MODEL), lambda bi, t: (bi, jnp.maximum(t * hb - 1, 0), 0)),
                  tile(D_MODEL),
                  pl.BlockSpec((1, A_HALO, D_MODEL), lambda bi, t: (bi, jnp.minimum((t + 1) * hb, l // A_HALO - 1), 0)),
                  sh, sc, const((1, D_MODEL)), const((D_MODEL, EVEN_IN)), const((A_CONV_W, SUBLANES, HALF)),
                  const((SUBLANES, HALF)), const((1, HALF)), const((1, HALF)), const((HALF, HALF)), const((1, HALF)),
                  const((HALF, 2 * HALF))],
        out_specs=[tile(HALF),
                   pl.BlockSpec((1, 2, tl, HALF), lambda bi, t: (bi, 0, t, 0)),
                   tile(HALF)],
        out_shape=[jax.ShapeDtypeStruct((b, l, HALF), BF16),
                   jax.ShapeDtypeStruct((b, 2, l, HALF), BF16),
                   jax.ShapeDtypeStruct((b, l, HALF), F32)],
        scratch_shapes=[pltpu.VMEM((SUBLANES, tl + 2 * A_HALO, HALF), F32), pltpu.VMEM((tl, HALF), F32)],
        compiler_params=_params("arbitrary", "arbitrary"),
        name="even_in",
    )(jnp.zeros((1,), jnp.int32), x, x, x, mod, mod, norm_g, w_in, conv_w, conv_b, ln_g, ln_b, pw_w, pw_b, gmat)

    assert l == RADIX ** 3
    y = _fft_4096(z)
    return ya, y, sgb


def _rope(t, cos_ref, sin_ref):
    reps = t.shape[1] // 128
    c = jnp.concatenate([cos_ref[...]] * reps, axis=1) if reps > 1 else cos_ref[...]
    s = jnp.concatenate([sin_ref[...]] * reps, axis=1) if reps > 1 else sin_ref[...]
    lane = lax.broadcasted_iota(jnp.int32, t.shape, 1)
    partner = jnp.where((lane & 16) == 0, pltpu.roll(t, t.shape[1] - 16, 1), pltpu.roll(t, 16, 1))
    return t * c + partner * s


def _pair_heads(t):
    n = N_Q_HEADS // 2
    vregs = [t[:, j * LANES:(j + 1) * LANES] for j in range(n)]
    swapped = [pltpu.roll(v, HEAD_DIM, 1) for v in vregs]
    low = lax.broadcasted_iota(jnp.int32, vregs[0].shape, 1) < HEAD_DIM
    out = []
    for g in range(n):
        a, b = g // 2, (g + n) // 2
        out.append(jnp.where(low, vregs[a], swapped[b]) if g % 2 == 0 else jnp.where(low, swapped[a], vregs[b]))
    return jnp.concatenate(out, axis=1)


def _even_out_odd_in_body(y, sgb, ya, x, gt_ref, wout_ref, sh_ref, sc_ref, g_ref, win_ref, cos_ref, sin_ref,
                          xo_ref, cbg_ref, z_ref, q_ref, k_ref, v_ref, sag_ref):
    yb = (y * sgb).astype(BF16)
    mix = _dot(jnp.concatenate([ya, yb], axis=1), wout_ref[...])
    x = x + gt_ref[...] * mix
    xo_ref[0] = x
    h = _pre_norm(x, g_ref, sh_ref, sc_ref).astype(BF16)
    q0 = 4 * HALF
    pa = _dot(h, win_ref[:, q0:ODD_IN])
    q = _rope(pa[:, 0:HALF], cos_ref, sin_ref) * Q_SCALE
    q_ref[0] = _pair_heads(q).astype(BF16)
    k_ref[0] = _rope(pa[:, HALF:HALF + KV_WIDTH], cos_ref, sin_ref).astype(BF16)
    v_ref[0] = pa[:, HALF + KV_WIDTH:HALF + 2 * KV_WIDTH].astype(BF16)
    sag_ref[0] = _pair_heads(jax.nn.silu(pa[:, HALF + 2 * KV_WIDTH:]))
    pc = _dot(h, win_ref[:, 0:q0])
    cbg_ref[0] = pc[:, 0:HALF] * jax.nn.silu(pc[:, 3 * HALF:4 * HALF])
    z_ref[0] = pc[:, HALF:2 * HALF] * pc[:, 2 * HALF:3 * HALF]


def _even_out_odd_in_kernel(y_ref, sgb_ref, ya_ref, x_ref, *refs):
    _even_out_odd_in_body(y_ref[0], sgb_ref[0], ya_ref[0], x_ref[0], *refs)


def _short_pair_front_kernel(zero_ref, x_ref, she_ref, sce_ref, gte_ref, ge_ref, wine_ref, cw_ref, cb_ref, lg_ref,
                             lb_ref, pw_ref, pb_ref, gm_ref, f_ref, woute_ref, sho_ref, sco_ref, go_ref, wino_ref,
                             cos_ref, sin_ref, xo_ref, cbg_ref, z_ref, q_ref, k_ref, v_ref, sag_ref, u_scr, c_scr, *,
                             tl):
    x = x_ref[0]
    xe = jnp.concatenate([x[:A_HALO], x, x[tl - A_HALO:]], axis=0)
    ya, pq, sgb = _even_in_body(xe, 0, zero_ref, she_ref, sce_ref, ge_ref, wine_ref, cw_ref, cb_ref, lg_ref, lb_ref,
                                pw_ref, pb_ref, gm_ref, u_scr, c_scr, tl=tl, nt=1)
    zri = jnp.concatenate([pq[:, :HALF].astype(BF16), pq[:, HALF:].astype(BF16)], axis=0)
    y = _dot(f_ref[...], zri)
    _even_out_odd_in_body(y, sgb, ya, x, gte_ref, woute_ref, sho_ref, sco_ref, go_ref, wino_ref, cos_ref, sin_ref,
                          xo_ref, cbg_ref, z_ref, q_ref, k_ref, v_ref, sag_ref)


def _short_pair_front(x, mod_e, norm_g, w_in, conv_w, conv_b, ln_g, ln_b, pw_w, pw_b, gmat, w_out, mod_o, norm_o,
                      wo_in, cos_t, sin_t, *, layer, mod_row):
    b, l, _ = x.shape
    she, sce, gte = _mod_specs(layer, mod_row)
    sho, sco, _ = _mod_specs(layer, mod_row)
    const = functools.partial(_layer_spec, layer)
    tile = lambda w: pl.BlockSpec((1, l, w), lambda bi, t: (bi, 0, 0))
    fixed = lambda shape: pl.BlockSpec(shape, lambda bi, t: (0,) * len(shape))
    k = np.arange(l)
    ang = 2.0 * np.pi * ((k[:, None] * k[None, :]) % l) / l
    f = jnp.asarray(np.concatenate([np.cos(ang), np.sin(ang)], axis=1) / np.sqrt(l), F32).astype(BF16)
    return pl.pallas_call(
        functools.partial(_short_pair_front_kernel, tl=l),
        grid=(b, 1),
        in_specs=[pl.BlockSpec(memory_space=pltpu.SMEM), tile(D_MODEL), she, sce, gte,
                  const((1, D_MODEL)), const((D_MODEL, EVEN_IN)), const((A_CONV_W, SUBLANES, HALF)),
                  const((SUBLANES, HALF)), const((1, HALF)), const((1, HALF)), const((HALF, HALF)), const((1, HALF)),
                  const((HALF, 2 * HALF)), fixed((l, 2 * l)), const((D_MODEL, D_MODEL)),
                  sho, sco, const((1, D_MODEL)), const((D_MODEL, ODD_IN)), fixed((l, 128)), fixed((l, 128))],
        out_specs=[tile(D_MODEL), tile(HALF), tile(HALF), tile(HALF), tile(KV_WIDTH), tile(KV_WIDTH), tile(HALF)],
        out_shape=[jax.ShapeDtypeStruct((b, l, D_MODEL), F32),
                   jax.ShapeDtypeStruct((b, l, HALF), F32), jax.ShapeDtypeStruct((b, l, HALF), F32),
                   jax.ShapeDtypeStruct((b, l, HALF), BF16), jax.ShapeDtypeStruct((b, l, KV_WIDTH), BF16),
                   jax.ShapeDtypeStruct((b, l, KV_WIDTH), BF16), jax.ShapeDtypeStruct((b, l, HALF), F32)],
        scratch_shapes=[pltpu.VMEM((SUBLANES, l + 2 * A_HALO, HALF), F32), pltpu.VMEM((l, HALF), F32)],
        compiler_params=_params("arbitrary", "arbitrary"),
        name="short_pair_front",
    )(jnp.zeros((1,), jnp.int32), x, mod_e, mod_e, mod_e, norm_g, w_in, conv_w, conv_b, ln_g, ln_b, pw_w, pw_b, gmat,
      f, w_out, mod_o, mod_o, norm_o, wo_in, cos_t, sin_t)


def _even_out_odd_in(y, sgb, ya, x, mod_e, w_out, mod_o, norm_g, w_in, cos_t, sin_t, *, layer, mod_row, tl):
    b, l, _ = x.shape
    _, _, gt = _mod_specs(layer, mod_row)
    sh, sc, _ = _mod_specs(layer, mod_row)
    tile = lambda w: pl.BlockSpec((1, tl, w), lambda bi, t: (bi, t, 0))
    const = functools.partial(_layer_spec, layer)
    return pl.pallas_call(
        _even_out_odd_in_kernel,
        grid=(b, l // tl),
        in_specs=[tile(HALF), tile(HALF), tile(HALF), tile(D_MODEL), gt, const((D_MODEL, D_MODEL)),
                  sh, sc, const((1, D_MODEL)), const((D_MODEL, ODD_IN)),
                  pl.BlockSpec((tl, 128), lambda bi, t: (t, 0)),
                  pl.BlockSpec((tl, 128), lambda bi, t: (t, 0))],
        out_specs=[tile(D_MODEL), tile(HALF), tile(HALF), tile(HALF), tile(KV_WIDTH), tile(KV_WIDTH), tile(HALF)],
        out_shape=[jax.ShapeDtypeStruct((b, l, D_MODEL), F32),
                   jax.ShapeDtypeStruct((b, l, HALF), F32), jax.ShapeDtypeStruct((b, l, HALF), F32),
                   jax.ShapeDtypeStruct((b, l, HALF), BF16), jax.ShapeDtypeStruct((b, l, KV_WIDTH), BF16),
                   jax.ShapeDtypeStruct((b, l, KV_WIDTH), BF16), jax.ShapeDtypeStruct((b, l, HALF), F32)],
        compiler_params=_params("arbitrary", "arbitrary"),
        name="even_out_odd_in",
    )(y, sgb, ya, x, mod_e, w_out, mod_o, mod_o, norm_g, w_in, cos_t, sin_t)


def _odd_out_kernel(*refs, layer, tq, nt, window, final):
    refs = list(refs)
    q_ref = refs.pop(0)
    if window:
        kp_ref, km_ref, kn_ref, vp_ref, vm_ref, vn_ref = refs[:6]
        refs = refs[6:]
    kc_ref, vc_ref, sink_ref, zp_ref, zm_ref, zn_ref, cbg_ref, sag_ref, x_ref, gt_ref, cw_ref, wout_ref = refs[:12]
    refs = refs[12:]
    fn_ref = refs.pop(0) if final else None
    o_ref = refs.pop(0)

    t = pl.program_id(1)
    nqb = tq // BLOCK
    lane = lax.broadcasted_iota(jnp.int32, (BLOCK, 128), 1)
    low = lane < HEAD_DIM
    kc, vc = kc_ref[0], vc_ref[0]
    if window:
        kext = jnp.concatenate([kp_ref[0], km_ref[0], kn_ref[0]], axis=0)
        vext = jnp.concatenate([vp_ref[0], vm_ref[0], vn_ref[0]], axis=0)
        qi = lax.broadcasted_iota(jnp.int32, (BLOCK, 3 * BLOCK), 0)
        kj = lax.broadcasted_iota(jnp.int32, (BLOCK, 3 * BLOCK), 1)
        band = (kj - qi >= 0) & (kj - qi <= 2 * BLOCK)

    att_rows = []
    zero = jnp.zeros((BLOCK, LANES), BF16)
    heads_per_kv = N_Q_HEADS // 2
    for qb in range(nqb):
        qg = q_ref[0, qb * BLOCK:(qb + 1) * BLOCK, :]
        slots = []
        for g in range(heads_per_kv):
            qq = qg[:, g * LANES:(g + 1) * LANES]
            slots += [jnp.where(low, qq, zero), jnp.where(low, zero, qq)]
        qs = jnp.concatenate(slots, axis=0)
        s_c = _dot_nt(qs, kc)
        if window:
            s_w = _dot_nt(qs, kext[qb * BLOCK:(qb + 3) * BLOCK])
            valid = band
            if qb == 0:
                valid = valid & ((kj >= BLOCK) | (t > 0))
            if qb == nqb - 1:
                valid = valid & ((kj < 2 * BLOCK) | (t < nt - 1))
        pws, pcs, dens = [], [], []
        for s in range(N_Q_HEADS):
            head = s // 2 + heads_per_kv * (s % 2)
            sink = sink_ref[layer, head] * LOG2E
            sc_s = s_c[s * BLOCK:(s + 1) * BLOCK]
            m = jnp.maximum(jnp.max(sc_s, axis=-1, keepdims=True), sink)
            if window:
                sw_s = jnp.where(valid, s_w[s * BLOCK:(s + 1) * BLOCK], NEG_INF)
                m = jnp.maximum(m, jnp.max(sw_s, axis=-1, keepdims=True))
            pc = jnp.exp2(sc_s - m)
            den = jnp.exp2(sink - m) + jnp.sum(pc, axis=-1, keepdims=True)
            if window:
                pw = jnp.exp2(sw_s - m)
                den = den + jnp.sum(pw, axis=-1, keepdims=True)
                pws.append(pw.astype(BF16))
            pcs.append(pc.astype(BF16))
            dens.append(den)
        o = _dot(jnp.concatenate(pcs, axis=0), vc)
        if window:
            o = o + _dot(jnp.concatenate(pws, axis=0), vext[qb * BLOCK:(qb + 3) * BLOCK])
        o = o / jnp.concatenate(dens, axis=0)
        att_rows.append(jnp.concatenate(
            [jnp.where(low, o[(2 * g) * BLOCK:(2 * g + 1) * BLOCK], o[(2 * g + 1) * BLOCK:(2 * g + 2) * BLOCK])
             for g in range(heads_per_kv)], axis=1))
    att = jnp.concatenate(att_rows, axis=0) if nqb > 1 else att_rows[0]
    yd = att * sag_ref[0]

    zp = jnp.where(t > 0, zp_ref[0], 0.0)
    zn = jnp.where(t < nt - 1, zn_ref[0], 0.0)
    zext = jnp.concatenate([zp, zm_ref[0], zn], axis=0)
    conv = (zext[C_HALO - 1:C_HALO - 1 + tq] * cw_ref[0:1, :] + zext[C_HALO:C_HALO + tq] * cw_ref[1:2, :]
            + zext[C_HALO + 1:C_HALO + 1 + tq] * cw_ref[2:3, :])
    yc = cbg_ref[0] * conv
    order = [g + (N_Q_HEADS // 2) * half for g in range(N_Q_HEADS // 2) for half in range(2)]
    w_att = jnp.concatenate([wout_ref[HALF + h * HEAD_DIM:HALF + (h + 1) * HEAD_DIM] for h in order], axis=0)
    mix = _dot(yc.astype(BF16), wout_ref[0:HALF]) + _dot(yd.astype(BF16), w_att)
    out = x_ref[0] + gt_ref[...] * mix
    if final:
        ms = jnp.mean(out * out, axis=-1, keepdims=True)
        out = out * lax.rsqrt(ms + EPS) * fn_ref[...]
    o_ref[0] = out


def _odd_out(x, mod, q, k, v, kc, vc, sink, z, cbg, sag, conv_w, w_out, final_norm, *, layer, mod_row, tq, window):
    b, l, _ = x.shape
    nt = l // tq
    nqb = tq // BLOCK
    lc = kc.shape[1]
    _, _, gt = _mod_specs(layer, mod_row)
    tile = lambda w: pl.BlockSpec((1, tq, w), lambda bi, t: (bi, t, 0))
    prev_blk = pl.BlockSpec((1, BLOCK, KV_WIDTH), lambda bi, t: (bi, jnp.maximum(t * nqb - 1, 0), 0))
    next_blk = pl.BlockSpec((1, BLOCK, KV_WIDTH), lambda bi, t: (bi, jnp.minimum((t + 1) * nqb, l // BLOCK - 1), 0))
    zh = tq // C_HALO
    args, specs = [q], [tile(HALF)]
    if window:
        args += [k, k, k, v, v, v]
        specs += [prev_blk, tile(KV_WIDTH), next_blk, prev_blk, tile(KV_WIDTH), next_blk]
    args += [kc, vc, sink, z, z, z, cbg, sag, x, mod, conv_w, w_out]
    specs += [pl.BlockSpec((1, lc, KV_WIDTH), lambda bi, t: (bi, 0, 0)),
              pl.BlockSpec((1, lc, KV_WIDTH), lambda bi, t: (bi, 0, 0)),
              pl.BlockSpec(memory_space=pltpu.SMEM),
              pl.BlockSpec((1, C_HALO, HALF), lambda bi, t: (bi, jnp.maximum(t * zh - 1, 0), 0)),
              tile(HALF),
              pl.BlockSpec((1, C_HALO, HALF), lambda bi, t: (bi, jnp.minimum((t + 1) * zh, l // C_HALO - 1), 0)),
              tile(HALF), tile(HALF), tile(D_MODEL), gt,
              _layer_spec(layer, (3, HALF)), _layer_spec(layer, (D_MODEL, D_MODEL))]
    if final_norm is not None:
        args.append(final_norm)
        specs.append(pl.BlockSpec((1, D_MODEL), lambda bi, t: (0, 0)))
    return pl.pallas_call(
        functools.partial(_odd_out_kernel, layer=layer, tq=tq, nt=nt, window=window, final=final_norm is not None),
        grid=(b, nt),
        in_specs=specs,
        out_specs=tile(D_MODEL),
        out_shape=jax.ShapeDtypeStruct((b, l, D_MODEL), F32),
        compiler_params=_params("arbitrary", "arbitrary"),
        name="odd_out_window" if window else "odd_out_context",
    )(*args)


def _rope_tables(l, grid_w):
    nf = HEAD_DIM // 4
    pos = np.arange(l)
    inv = np.power(ROPE_BASE, -np.arange(nf, dtype=np.float64) / nf)
    ang = np.stack([(pos // grid_w)[:, None] * inv, (pos % grid_w)[:, None] * inv], axis=1)
    cos, sin = np.cos(ang), np.sin(ang)
    cos_t = np.stack([cos, cos], axis=2).reshape(l, HEAD_DIM)
    sin_t = np.stack([-sin, sin], axis=2).reshape(l, HEAD_DIM)
    return jnp.asarray(np.tile(cos_t, (1, 2)), F32), jnp.asarray(np.tile(sin_t, (1, 2)), F32)


X_TILE = 512
GRID_W = 64


def kernel(x, c, ctx, c_ctx, e_norm, e_w_mod, e_b_mod, e_w_in, e_a_conv_w, e_a_conv_b, e_a_ln_g, e_a_ln_b,
           e_a_pw_w, e_a_pw_b, e_b_fw, e_w_out, o_norm, o_w_mod, o_b_mod, o_w_in, o_c_conv_w, o_sink, o_w_out,
           final_norm):
    b, l, _ = x.shape
    lc = ctx.shape[1]
    assert x.shape == (8, 4096, D_MODEL) and ctx.shape == (8, 256, D_MODEL)
    depth = e_norm.shape[0] + o_norm.shape[0]
    assert depth % 2 == 0

    cv = jnp.zeros((MOD_ROWS, D_MODEL), F32).at[:b].set(c).at[CTX_ROW].set(c_ctx)
    mod_e = _modulation(cv, e_w_mod, e_b_mod)
    mod_o = _modulation(cv, o_w_mod, o_b_mod)
    gmat = _fourier_channel_mats(e_b_fw)

    cos_x, sin_x = _rope_tables(l, GRID_W)
    cos_c, sin_c = jnp.ones((lc, 128), F32), jnp.zeros((lc, 128), F32)

    row = lambda a: a.reshape(a.shape[0], 1, a.shape[-1])
    rep = lambda a: jnp.broadcast_to(a[..., None, :], a.shape[:-1] + (SUBLANES, HALF))
    front = (row(e_norm), e_w_in.astype(BF16), rep(e_a_conv_w), rep(e_a_conv_b), row(e_a_ln_g), row(e_a_ln_b),
             e_a_pw_w.astype(BF16), row(e_a_pw_b), gmat)
    we_out, wo_in, wo_out = e_w_out.astype(BF16), o_w_in.astype(BF16), o_w_out.astype(BF16)
    sink = o_sink
    norm_o = row(o_norm)
    fin = final_norm.reshape(1, D_MODEL)

    xc = ctx
    for i in range(depth // 2):
        last = i == depth // 2 - 1
        xc, cbg_c, z_c, q_c, k_c, v_c, sag_c = _short_pair_front(
            xc, mod_e, *front, we_out, mod_o, norm_o, wo_in, cos_c, sin_c, layer=i, mod_row=CTX_ROW)
        ya, y, sgb = _even_front(x, mod_e, *front, layer=i, mod_row=None, tl=X_TILE)
        x, cbg, z, q, k, v, sag = _even_out_odd_in(
            y, sgb, ya, x, mod_e, we_out, mod_o, norm_o, wo_in, cos_x, sin_x, layer=i, mod_row=None, tl=X_TILE)
        x = _odd_out(x, mod_o, q, k, v, k_c, v_c, sink, z, cbg, sag, o_c_conv_w, wo_out, fin if last else None,
                     layer=i, mod_row=None, tq=X_TILE, window=True)
        if not last:
            xc = _odd_out(xc, mod_o, q_c, None, None, k_c, v_c, sink, z_c, cbg_c, sag_c, o_c_conv_w, wo_out, None,
                          layer=i, mod_row=CTX_ROW, tq=lc, window=False)
    return x
```

```python
import functools

import numpy as np
import jax
import jax.numpy as jnp
from jax import lax
from jax.experimental import pallas as pl
from jax.experimental.pallas import tpu as pltpu

D_MODEL = 1024
HALF = 512
A_CONV_W = 31
A_HALO = 16
C_HALO = 8
SUBLANES = 8
B_GROUPS = 4
B_GROUP_DIM = 128
HEAD_DIM = 64
N_Q_HEADS = 8
KV_WIDTH = 128
BLOCK = 128
ODD_IN = 3328
EVEN_IN = 2560
ROPE_BASE = 10000.0
NEG_INF = -1e30
LOG2E = 1.4426950408889634
Q_SCALE = HEAD_DIM ** -0.5 * LOG2E
EPS = 1e-6
RADIX = 16
MOD_ROWS = 16
CTX_ROW = 8
LANES = 128
MXU_COLS = 256
MXU_ROWS = 16
FFT_COLS = MXU_COLS
PM_SPLIT = ((), (0,), (1, 2), (3,))
PM_TAIL = (4, 5)
PM_ORDER = tuple(i for grp in PM_SPLIT for i in grp) + PM_TAIL
F32 = jnp.float32
BF16 = jnp.bfloat16
VMEM_LIMIT_BYTES = 48 * 1024 * 1024


def _params(*sem):
    return pltpu.CompilerParams(dimension_semantics=sem, vmem_limit_bytes=VMEM_LIMIT_BYTES)


def _dot(a, b):
    return jnp.dot(a, b, preferred_element_type=F32)


def _dot_nt(a, b):
    return lax.dot_general(a, b, (((1,), (1,)), ((), ())), preferred_element_type=F32)


def _pre_norm(x, g_ref, sh_ref, sc_ref):
    mult = g_ref[...] * (1.0 + sc_ref[...])
    ms = jnp.mean(x * x, axis=-1, keepdims=True)
    return (x * lax.rsqrt(ms + EPS)) * mult + sh_ref[...]


def _layer_spec(layer, shape):
    return pl.BlockSpec((None,) + tuple(shape), lambda bi, t: (layer,) + (0,) * len(shape))


def _mod_kernel(c_ref, w_ref, b_ref, o_ref):
    o_ref[0, :, 0, :] = _dot(jax.nn.silu(c_ref[...]), w_ref[0]) + b_ref[0]


def _modulation(cv, w_mod, b_mod):
    n = w_mod.shape[0]
    return pl.pallas_call(
        _mod_kernel,
        grid=(n, 3),
        in_specs=[pl.BlockSpec((MOD_ROWS, D_MODEL), lambda l, j: (0, 0)),
                  pl.BlockSpec((1, D_MODEL, D_MODEL), lambda l, j: (l, 0, j)),
                  pl.BlockSpec((1, 1, D_MODEL), lambda l, j: (l, 0, j))],
        out_specs=pl.BlockSpec((1, MOD_ROWS, 1, D_MODEL), lambda l, j: (l, 0, 0, j)),
        out_shape=jax.ShapeDtypeStruct((n, MOD_ROWS, 1, 3 * D_MODEL), F32),
        compiler_params=_params("arbitrary", "arbitrary"),
        name="modulation",
    )(cv, w_mod, b_mod.reshape(n, 1, 3 * D_MODEL))


def _mod_specs(layer, row):
    def spec(j):
        if row is None:
            return pl.BlockSpec((None, None, 1, D_MODEL), lambda b, t: (layer, b, 0, j))
        return pl.BlockSpec((None, None, 1, D_MODEL), lambda b, t: (layer, row, 0, j))
    return spec(0), spec(1), spec(2)


def _gprep_kernel(cd_ref, sd_ref, fw_ref, o_ref):
    o_ref[...] = jnp.zeros(o_ref.shape, o_ref.dtype)
    for g in range(B_GROUPS):
        fw = fw_ref[0, g]
        gc = jnp.dot(cd_ref[...], fw, preferred_element_type=F32, precision=lax.Precision.HIGHEST)
        gs = jnp.dot(sd_ref[...], fw, preferred_element_type=F32, precision=lax.Precision.HIGHEST)
        lo, hi = g * B_GROUP_DIM, (g + 1) * B_GROUP_DIM
        o_ref[0, lo:hi, lo:hi] = gc.astype(BF16)
        o_ref[0, lo:hi, HALF + lo:HALF + hi] = (-gs).astype(BF16)


def _fourier_channel_mats(fw):
    n = fw.shape[0]
    d = np.arange(B_GROUP_DIM)
    ang = 2.0 * np.pi * ((d[:, None] * d[None, :]) % B_GROUP_DIM) / B_GROUP_DIM
    cd = jnp.asarray(np.cos(ang) / np.sqrt(B_GROUP_DIM), F32)
    sd = jnp.asarray(np.sin(ang) / np.sqrt(B_GROUP_DIM), F32)
    mat = pl.BlockSpec((B_GROUP_DIM, B_GROUP_DIM), lambda i: (0, 0))
    return pl.pallas_call(
        _gprep_kernel,
        grid=(n,),
        in_specs=[mat, mat,
                  pl.BlockSpec((1, B_GROUPS, B_GROUP_DIM, B_GROUP_DIM), lambda i: (i, 0, 0, 0))],
        out_specs=pl.BlockSpec((1, HALF, 2 * HALF), lambda i: (i, 0, 0)),
        out_shape=jax.ShapeDtypeStruct((n, HALF, 2 * HALF), BF16),
        compiler_params=_params("arbitrary"),
        name="fourier_channel_mats",
    )(cd, sd, fw)


def _real_stack(m, real_only=False):
    top = np.concatenate([m.real, -m.imag], axis=1)
    if real_only:
        return top
    return np.concatenate([top, np.concatenate([m.imag, m.real], axis=1)], axis=0)


def _fft_tables_4096():
    r = np.arange(RADIX)
    eye = np.eye(RADIX)
    w16 = np.exp(-2j * np.pi * np.outer(r, r) / 16)
    w256 = np.exp(-2j * np.pi * np.outer(r, r) / 256)
    w4096 = np.exp(-2j * np.pi * np.outer(r, r) / 4096)
    ma = np.einsum("ka,kc,cd->kcad", w16, w4096, eye).reshape(256, 256) / 64.0
    mb = np.einsum("kb,kc,cd->kcbd", w16, w256, eye).reshape(256, 256)
    mc = np.einsum("kc,ae->kaec", w16, eye).reshape(256, 256)
    ph = 2.0 * np.pi * np.outer(r, r) / 256
    as_bf16 = lambda m: jnp.asarray(m, F32).astype(BF16)
    return (as_bf16(_real_stack(ma)), as_bf16(_real_stack(mb)), as_bf16(_real_stack(mc, real_only=True)),
            jnp.asarray(np.cos(ph), F32), jnp.asarray(np.sin(ph), F32))


def _fft_4096_kernel(z_ref, ma_ref, mb_ref, mc_ref, tc_ref, ts_ref, o_ref, s_scr):
    ct = z_ref.shape[-1]
    blk = RADIX * RADIX
    for b in range(RADIX):
        a1 = _dot(ma_ref[...], z_ref[0, :, :, b].reshape(2 * blk, ct))
        ar, ai = a1[:blk], a1[blk:]
        out_r, out_i = [], []
        for ka in range(RADIX):
            rr, ii = ar[ka * RADIX:(ka + 1) * RADIX], ai[ka * RADIX:(ka + 1) * RADIX]
            if b * ka == 0:
                out_r.append(rr)
                out_i.append(ii)
            else:
                c, s = tc_ref[b, ka], ts_ref[b, ka]
                out_r.append(rr * c + ii * s)
                out_i.append(ii * c - rr * s)
        s_scr[0, :, b] = jnp.concatenate(out_r, axis=0).astype(BF16).reshape(RADIX, RADIX, ct)
        s_scr[1, :, b] = jnp.concatenate(out_i, axis=0).astype(BF16).reshape(RADIX, RADIX, ct)
    for ka in range(RADIX):
        a2 = _dot(mb_ref[...], s_scr[:, ka].reshape(2 * blk, ct))
        s_scr[:, ka] = a2.astype(BF16).reshape(2, RADIX, RADIX, ct)
    for kb in range(RADIX):
        y = _dot(mc_ref[...], s_scr[:, :, kb].reshape(2 * blk, ct))
        o_ref[0, :, kb] = y.reshape(RADIX, RADIX, ct)


def _fft_4096(z):
    b = z.shape[0]
    ma, mb, mc, tc, ts = _fft_tables_4096()
    blk = RADIX * RADIX
    mat = lambda rows: pl.BlockSpec((rows, 2 * blk), lambda bi, j: (0, 0))
    smem = pl.BlockSpec(memory_space=pltpu.SMEM)
    y = pl.pallas_call(
        _fft_4096_kernel,
        grid=(b, HALF // FFT_COLS),
        in_specs=[pl.BlockSpec((1, 2, RADIX, RADIX, RADIX, FFT_COLS), lambda bi, j: (bi, 0, 0, 0, 0, j)),
                  mat(2 * blk), mat(2 * blk), mat(blk), smem, smem],
        out_specs=pl.BlockSpec((1, RADIX, RADIX, RADIX, FFT_COLS), lambda bi, j: (bi, 0, 0, 0, j)),
        out_shape=jax.ShapeDtypeStruct((b, RADIX, RADIX, RADIX, HALF), F32),
        scratch_shapes=[pltpu.VMEM((2, RADIX, RADIX, RADIX, FFT_COLS), BF16)],
        compiler_params=_params("arbitrary", "arbitrary"),
        name="fft_4096",
    )(z.reshape(b, 2, RADIX, RADIX, RADIX, HALF), ma, mb, mc, tc, ts)
    return y.reshape(b, RADIX ** 3, HALF)


def _zeros_once_computed(tiles, zero, rows):
    acc = None
    for tile in tiles:
        bits = lax.bitcast_convert_type(tile, jnp.int32)
        bits = bits.reshape(tile.shape[0] // SUBLANES, SUBLANES, tile.shape[1])
        for i in range(0, bits.shape[0], MXU_ROWS // SUBLANES):
            for c in range(0, tile.shape[1], MXU_COLS):
                part = bits[i, :, c:c + LANES]
                acc = part if acc is None else acc | part
    z = lax.bitcast_convert_type(acc & zero, F32)
    return jnp.broadcast_to(z[None], (rows // SUBLANES, SUBLANES, LANES)).reshape(rows, LANES)


def _even_in_body(xe, t, zero_ref, sh_ref, sc_ref, g_ref, win_ref, cw_ref, cb_ref, lg_ref, lb_ref, pw_ref, pb_ref,
                  gm_ref, u_scr, c_scr, *, tl, nt):
    ext = tl + 2 * A_HALO
    he = _pre_norm(xe, g_ref, sh_ref, sc_ref).astype(BF16)
    row = lax.broadcasted_iota(jnp.int32, (ext, 1), 0)
    inside = ((row >= A_HALO) | (t > 0)) & ((row < tl + A_HALO) | (t < nt - 1))
    sub = lax.broadcasted_iota(jnp.int32, (ext // SUBLANES, SUBLANES, LANES), 1)
    rc = 128
    off = A_HALO - (A_CONV_W - 1) // 2
    hm = he[A_HALO:A_HALO + tl]
    pm_tiles = []
    pad = jnp.zeros((ext, LANES), F32)
    for blk in range(HALF // LANES):
        lanes = slice(blk * LANES, (blk + 1) * LANES)
        w_blk = jnp.concatenate([win_ref[:, lanes], win_ref[:, HALF + blk * LANES:HALF + (blk + 1) * LANES]], axis=1)
        pa = _dot(he, w_blk)
        u = jnp.where(inside, pa[:, :LANES] * jax.nn.sigmoid(pa[:, LANES:]), pad)
        if PM_SPLIT[blk]:
            new = [_dot(hm, win_ref[:, 2 * HALF + i * MXU_COLS:2 * HALF + (i + 1) * MXU_COLS])
                   for i in PM_SPLIT[blk]]
            pm_tiles += new
            pad = _zeros_once_computed(new, zero_ref[0], ext)
        u_scr[0, :, lanes] = u
        u3 = u.reshape(ext // SUBLANES, SUBLANES, LANES)
        for r in range(1, SUBLANES):
            rot = pltpu.roll(u3, SUBLANES - r, 1)
            nxt = jnp.concatenate([rot[1:], rot[:1]], axis=0)
            u_scr[r, :, lanes] = jnp.where(sub < SUBLANES - r, rot, nxt).reshape(ext, LANES)
        for r0 in range(0, tl, rc):
            acc = jnp.broadcast_to(cb_ref[:, lanes][None], (rc // SUBLANES, SUBLANES, LANES))
            for j in range(A_CONV_W):
                q, r = divmod(off + j, SUBLANES)
                uj = u_scr[r, r0 + SUBLANES * q:r0 + SUBLANES * q + rc, lanes]
                acc = acc + uj.reshape(rc // SUBLANES, SUBLANES, LANES) * cw_ref[j, :, lanes][None]
            c_scr[r0:r0 + rc, lanes] = acc.reshape(rc, LANES)
    pm_tiles += [_dot(hm, win_ref[:, 2 * HALF + i * MXU_COLS:2 * HALF + (i + 1) * MXU_COLS]) for i in PM_TAIL]
    pm = jnp.concatenate([pm_tiles[PM_ORDER.index(i)] for i in range(len(PM_ORDER))], axis=1)
    pq = _dot(pm[:, HALF:2 * HALF].astype(BF16), gm_ref[...])
    tie = _zeros_once_computed(pm_tiles[-len(PM_TAIL):] + [pq], zero_ref[0], tl)
    cv = c_scr[...] + jnp.concatenate([tie] * (HALF // LANES), axis=1)
    mu = jnp.mean(cv, axis=-1, keepdims=True)
    cen = cv - mu
    var = jnp.mean(cen * cen, axis=-1, keepdims=True)
    ln = cen * lax.rsqrt(var + EPS) * lg_ref[...] + lb_ref[...]
    ya = (_dot(jax.nn.silu(ln).astype(BF16), pw_ref[...]) + pb_ref[...]) * jax.nn.silu(pm[:, :HALF])
    return ya.astype(BF16), pq, jax.nn.silu(pm[:, 2 * HALF:])


def _even_in_kernel(zero_ref, xp_ref, xm_ref, xn_ref, sh_ref, sc_ref, g_ref, win_ref, cw_ref, cb_ref, lg_ref, lb_ref,
                    pw_ref, pb_ref, gm_ref, ya_ref, z_ref, sgb_ref, u_scr, c_scr, *, tl, nt):
    xe = jnp.concatenate([xp_ref[0], xm_ref[0], xn_ref[0]], axis=0)
    ya, pq, sgb = _even_in_body(xe, pl.program_id(1), zero_ref, sh_ref, sc_ref, g_ref, win_ref, cw_ref, cb_ref,
                                lg_ref, lb_ref, pw_ref, pb_ref, gm_ref, u_scr, c_scr, tl=tl, nt=nt)
    ya_ref[0] = ya
    z_ref[0, 0] = pq[:, :HALF].astype(BF16)
    z_ref[0, 1] = pq[:, HALF:].astype(BF16)
    sgb_ref[0] = sgb


def _even_front(x, mod, norm_g, w_in, conv_w, conv_b, ln_g, ln_b, pw_w, pw_b, gmat, *, layer, mod_row, tl):
    b, l, _ = x.shape
    nt = l // tl
    hb = tl // A_HALO
    sh, sc, _ = _mod_specs(layer, mod_row)
    const = functools.partial(_layer_spec, layer)
    tile = lambda w: pl.BlockSpec((1, tl, w), lambda bi, t: (bi, t, 0))
    ya, z, sgb = pl.pallas_call(
        functools.partial(_even_in_kernel, tl=tl, nt=nt),
        grid=(b, nt),
        in_specs=[pl.BlockSpec(memory_space=pltpu.SMEM),
                  pl.BlockSpec((1, A_HALO, D_MODEL), lambda bi, t: (bi, jnp.maximum(t * hb - 1, 0), 0)),
                  tile(D_MODEL),
                  pl.BlockSpec((1, A_HALO, D_MODEL), lambda bi, t: (bi, jnp.minimum((t + 1) * hb, l // A_HALO - 1), 0)),
                  sh, sc, const((1, D_MODEL)), const((D_MODEL, EVEN_IN)), const((A_CONV_W, SUBLANES, HALF)),
                  const((SUBLANES, HALF)), const((1, HALF)), const((1, HALF)), const((HALF, HALF)), const((1, HALF)),
                  const((HALF, 2 * HALF))],
        out_specs=[tile(HALF),
                   pl.BlockSpec((1, 2, tl, HALF), lambda bi, t: (bi, 0, t, 0)),
                   tile(HALF)],
        out_shape=[jax.ShapeDtypeStruct((b, l, HALF), BF16),
                   jax.ShapeDtypeStruct((b, 2, l, HALF), BF16),
                   jax.ShapeDtypeStruct((b, l, HALF), F32)],
        scratch_shapes=[pltpu.VMEM((SUBLANES, tl + 2 * A_HALO, HALF), F32), pltpu.VMEM((tl, HALF), F32)],
        compiler_params=_params("arbitrary", "arbitrary"),
        name="even_in",
    )(jnp.zeros((1,), jnp.int32), x, x, x, mod, mod, norm_g, w_in, conv_w, conv_b, ln_g, ln_b, pw_w, pw_b, gmat)

    assert l == RADIX ** 3
    y = _fft_4096(z)
    return ya, y, sgb


def _rope(t, cos_ref, sin_ref):
    reps = t.shape[1] // 128
    c = jnp.concatenate([cos_ref[...]] * reps, axis=1) if reps > 1 else cos_ref[...]
    s = jnp.concatenate([sin_ref[...]] * reps, axis=1) if reps > 1 else sin_ref[...]
    lane = lax.broadcasted_iota(jnp.int32, t.shape, 1)
    partner = jnp.where((lane & 16) == 0, pltpu.roll(t, t.shape[1] - 16, 1), pltpu.roll(t, 16, 1))
    return t * c + partner * s


def _pair_heads(t):
    n = N_Q_HEADS // 2
    vregs = [t[:, j * LANES:(j + 1) * LANES] for j in range(n)]
    swapped = [pltpu.roll(v, HEAD_DIM, 1) for v in vregs]
    low = lax.broadcasted_iota(jnp.int32, vregs[0].shape, 1) < HEAD_DIM
    out = []
    for g in range(n):
        a, b = g // 2, (g + n) // 2
        out.append(jnp.where(low, vregs[a], swapped[b]) if g % 2 == 0 else jnp.where(low, swapped[a], vregs[b]))
    return jnp.concatenate(out, axis=1)


def _even_out_odd_in_body(y, sgb, ya, x, gt_ref, wout_ref, sh_ref, sc_ref, g_ref, win_ref, cos_ref, sin_ref,
                          xo_ref, cbg_ref, z_ref, q_ref, k_ref, v_ref, sag_ref):
    yb = (y * sgb).astype(BF16)
    cat = jnp.concatenate([ya, yb], axis=1)
    hr = x.shape[0] // 2
    xs, hs = [], []
    for rows in (slice(0, hr), slice(hr, 2 * hr)):
        xr = x[rows] + gt_ref[...] * _dot(cat[rows], wout_ref[...])
        xs.append(xr)
        hs.append(_pre_norm(xr, g_ref, sh_ref, sc_ref).astype(BF16))
    xo_ref[0] = jnp.concatenate(xs, axis=0)
    h = jnp.concatenate(hs, axis=0)
    q0 = 4 * HALF
    pa = _dot(h, win_ref[:, q0:ODD_IN])
    q = _rope(pa[:, 0:HALF], cos_ref, sin_ref) * Q_SCALE
    q_ref[0] = _pair_heads(q).astype(BF16)
    k_ref[0] = _rope(pa[:, HALF:HALF + KV_WIDTH], cos_ref, sin_ref).astype(BF16)
    v_ref[0] = pa[:, HALF + KV_WIDTH:HALF + 2 * KV_WIDTH].astype(BF16)
    sag_ref[0] = _pair_heads(jax.nn.silu(pa[:, HALF + 2 * KV_WIDTH:]))
    pc = _dot(h, win_ref[:, 0:q0])
    cbg_ref[0] = pc[:, 0:HALF] * jax.nn.silu(pc[:, 3 * HALF:4 * HALF])
    z_ref[0] = pc[:, HALF:2 * HALF] * pc[:, 2 * HALF:3 * HALF]


def _even_out_odd_in_kernel(y_ref, sgb_ref, ya_ref, x_ref, *refs):
    _even_out_odd_in_body(y_ref[0], sgb_ref[0], ya_ref[0], x_ref[0], *refs)


def _short_pair_front_kernel(zero_ref, x_ref, she_ref, sce_ref, gte_ref, ge_ref, wine_ref, cw_ref, cb_ref, lg_ref,
                             lb_ref, pw_ref, pb_ref, gm_ref, f_ref, woute_ref, sho_ref, sco_ref, go_ref, wino_ref,
                             cos_ref, sin_ref, xo_ref, cbg_ref, z_ref, q_ref, k_ref, v_ref, sag_ref, u_scr, c_scr, *,
                             tl):
    x = x_ref[0]
    xe = jnp.concatenate([x[:A_HALO], x, x[tl - A_HALO:]], axis=0)
    ya, pq, sgb = _even_in_body(xe, 0, zero_ref, she_ref, sce_ref, ge_ref, wine_ref, cw_ref, cb_ref, lg_ref, lb_ref,
                                pw_ref, pb_ref, gm_ref, u_scr, c_scr, tl=tl, nt=1)
    zri = jnp.concatenate([pq[:, :HALF].astype(BF16), pq[:, HALF:].astype(BF16)], axis=0)
    y = _dot(f_ref[...], zri)
    _even_out_odd_in_body(y, sgb, ya, x, gte_ref, woute_ref, sho_ref, sco_ref, go_ref, wino_ref, cos_ref, sin_ref,
                          xo_ref, cbg_ref, z_ref, q_ref, k_ref, v_ref, sag_ref)


def _short_pair_front(x, mod_e, norm_g, w_in, conv_w, conv_b, ln_g, ln_b, pw_w, pw_b, gmat, w_out, mod_o, norm_o,
                      wo_in, cos_t, sin_t, *, layer, mod_row):
    b, l, _ = x.shape
    she, sce, gte = _mod_specs(layer, mod_row)
    sho, sco, _ = _mod_specs(layer, mod_row)
    const = functools.partial(_layer_spec, layer)
    tile = lambda w: pl.BlockSpec((1, l, w), lambda bi, t: (bi, 0, 0))
    fixed = lambda shape: pl.BlockSpec(shape, lambda bi, t: (0,) * len(shape))
    k = np.arange(l)
    ang = 2.0 * np.pi * ((k[:, None] * k[None, :]) % l) / l
    f = jnp.asarray(np.concatenate([np.cos(ang), np.sin(ang)], axis=1) / np.sqrt(l), F32).astype(BF16)
    return pl.pallas_call(
        functools.partial(_short_pair_front_kernel, tl=l),
        grid=(b, 1),
        in_specs=[pl.BlockSpec(memory_space=pltpu.SMEM), tile(D_MODEL), she, sce, gte,
                  const((1, D_MODEL)), const((D_MODEL, EVEN_IN)), const((A_CONV_W, SUBLANES, HALF)),
                  const((SUBLANES, HALF)), const((1, HALF)), const((1, HALF)), const((HALF, HALF)), const((1, HALF)),
                  const((HALF, 2 * HALF)), fixed((l, 2 * l)), const((D_MODEL, D_MODEL)),
                  sho, sco, const((1, D_MODEL)), const((D_MODEL, ODD_IN)), fixed((l, 128)), fixed((l, 128))],
        out_specs=[tile(D_MODEL), tile(HALF), tile(HALF), tile(HALF), tile(KV_WIDTH), tile(KV_WIDTH), tile(HALF)],
        out_shape=[jax.ShapeDtypeStruct((b, l, D_MODEL), F32),
                   jax.ShapeDtypeStruct((b, l, HALF), F32), jax.ShapeDtypeStruct((b, l, HALF), F32),
                   jax.ShapeDtypeStruct((b, l, HALF), BF16), jax.ShapeDtypeStruct((b, l, KV_WIDTH), BF16),
                   jax.ShapeDtypeStruct((b, l, KV_WIDTH), BF16), jax.ShapeDtypeStruct((b, l, HALF), F32)],
        scratch_shapes=[pltpu.VMEM((SUBLANES, l + 2 * A_HALO, HALF), F32), pltpu.VMEM((l, HALF), F32)],
        compiler_params=_params("arbitrary", "arbitrary"),
        name="short_pair_front",
    )(jnp.zeros((1,), jnp.int32), x, mod_e, mod_e, mod_e, norm_g, w_in, conv_w, conv_b, ln_g, ln_b, pw_w, pw_b, gmat,
      f, w_out, mod_o, mod_o, norm_o, wo_in, cos_t, sin_t)


def _even_out_odd_in(y, sgb, ya, x, mod_e, w_out, mod_o, norm_g, w_in, cos_t, sin_t, *, layer, mod_row, tl):
    b, l, _ = x.shape
    _, _, gt = _mod_specs(layer, mod_row)
    sh, sc, _ = _mod_specs(layer, mod_row)
    tile = lambda w: pl.BlockSpec((1, tl, w), lambda bi, t: (bi, t, 0))
    const = functools.partial(_layer_spec, layer)
    return pl.pallas_call(
        _even_out_odd_in_kernel,
        grid=(b, l // tl),
        in_specs=[tile(HALF), tile(HALF), tile(HALF), tile(D_MODEL), gt, const((D_MODEL, D_MODEL)),
                  sh, sc, const((1, D_MODEL)), const((D_MODEL, ODD_IN)),
                  pl.BlockSpec((tl, 128), lambda bi, t: (t, 0)),
                  pl.BlockSpec((tl, 128), lambda bi, t: (t, 0))],
        out_specs=[tile(D_MODEL), tile(HALF), tile(HALF), tile(HALF), tile(KV_WIDTH), tile(KV_WIDTH), tile(HALF)],
        out_shape=[jax.ShapeDtypeStruct((b, l, D_MODEL), F32),
                   jax.ShapeDtypeStruct((b, l, HALF), F32), jax.ShapeDtypeStruct((b, l, HALF), F32),
                   jax.ShapeDtypeStruct((b, l, HALF), BF16), jax.ShapeDtypeStruct((b, l, KV_WIDTH), BF16),
                   jax.ShapeDtypeStruct((b, l, KV_WIDTH), BF16), jax.ShapeDtypeStruct((b, l, HALF), F32)],
        compiler_params=_params("arbitrary", "arbitrary"),
        name="even_out_odd_in",
    )(y, sgb, ya, x, mod_e, w_out, mod_o, mod_o, norm_g, w_in, cos_t, sin_t)


def _odd_out_kernel(*refs, layer, tq, nt, window, final):
    refs = list(refs)
    q_ref = refs.pop(0)
    if window:
        kp_ref, km_ref, kn_ref, vp_ref, vm_ref, vn_ref = refs[:6]
        refs = refs[6:]
    kc_ref, vc_ref, sink_ref, zp_ref, zm_ref, zn_ref, cbg_ref, sag_ref, x_ref, gt_ref, cw_ref, wout_ref = refs[:12]
    refs = refs[12:]
    fn_ref = refs.pop(0) if final else None
    o_ref = refs.pop(0)

    t = pl.program_id(1)
    nqb = tq // BLOCK
    lane = lax.broadcasted_iota(jnp.int32, (BLOCK, 128), 1)
    low = lane < HEAD_DIM
    kc, vc = kc_ref[0], vc_ref[0]
    if window:
        kext = jnp.concatenate([kp_ref[0], km_ref[0], kn_ref[0]], axis=0)
        vext = jnp.concatenate([vp_ref[0], vm_ref[0], vn_ref[0]], axis=0)
        qi = lax.broadcasted_iota(jnp.int32, (BLOCK, 3 * BLOCK), 0)
        kj = lax.broadcasted_iota(jnp.int32, (BLOCK, 3 * BLOCK), 1)
        band = (kj - qi >= 0) & (kj - qi <= 2 * BLOCK)

    att_rows = []
    zero = jnp.zeros((BLOCK, LANES), BF16)
    heads_per_kv = N_Q_HEADS // 2
    for qb in range(nqb):
        qg = q_ref[0, qb * BLOCK:(qb + 1) * BLOCK, :]
        slots = []
        for g in range(heads_per_kv):
            qq = qg[:, g * LANES:(g + 1) * LANES]
            slots += [jnp.where(low, qq, zero), jnp.where(low, zero, qq)]
        qs = jnp.concatenate(slots, axis=0)
        s_c = _dot_nt(qs, kc)
        if window:
            s_w = _dot_nt(qs, kext[qb * BLOCK:(qb + 3) * BLOCK])
            valid = band
            if qb == 0:
                valid = valid & ((kj >= BLOCK) | (t > 0))
            if qb == nqb - 1:
                valid = valid & ((kj < 2 * BLOCK) | (t < nt - 1))
        pws, pcs, dens = [], [], []
        for s in range(N_Q_HEADS):
            head = s // 2 + heads_per_kv * (s % 2)
            sink = sink_ref[layer, head] * LOG2E
            sc_s = s_c[s * BLOCK:(s + 1) * BLOCK]
            m = jnp.maximum(jnp.max(sc_s, axis=-1, keepdims=True), sink)
            if window:
                sw_s = jnp.where(valid, s_w[s * BLOCK:(s + 1) * BLOCK], NEG_INF)
                m = jnp.maximum(m, jnp.max(sw_s, axis=-1, keepdims=True))
            pc = jnp.exp2(sc_s - m)
            den = jnp.exp2(sink - m) + jnp.sum(pc, axis=-1, keepdims=True)
            if window:
                pw = jnp.exp2(sw_s - m)
                den = den + jnp.sum(pw, axis=-1, keepdims=True)
                pws.append(pw.astype(BF16))
            pcs.append(pc.astype(BF16))
            dens.append(den)
        o = _dot(jnp.concatenate(pcs, axis=0), vc)
        if window:
            o = o + _dot(jnp.concatenate(pws, axis=0), vext[qb * BLOCK:(qb + 3) * BLOCK])
        o = o / jnp.concatenate(dens, axis=0)
        att_rows.append(jnp.concatenate(
            [jnp.where(low, o[(2 * g) * BLOCK:(2 * g + 1) * BLOCK], o[(2 * g + 1) * BLOCK:(2 * g + 2) * BLOCK])
             for g in range(heads_per_kv)], axis=1))
    att = jnp.concatenate(att_rows, axis=0) if nqb > 1 else att_rows[0]
    yd = att * sag_ref[0]

    zp = jnp.where(t > 0, zp_ref[0], 0.0)
    zn = jnp.where(t < nt - 1, zn_ref[0], 0.0)
    zext = jnp.concatenate([zp, zm_ref[0], zn], axis=0)
    conv = (zext[C_HALO - 1:C_HALO - 1 + tq] * cw_ref[0:1, :] + zext[C_HALO:C_HALO + tq] * cw_ref[1:2, :]
            + zext[C_HALO + 1:C_HALO + 1 + tq] * cw_ref[2:3, :])
    yc = cbg_ref[0] * conv
    order = [g + (N_Q_HEADS // 2) * half for g in range(N_Q_HEADS // 2) for half in range(2)]
    w_att = jnp.concatenate([wout_ref[HALF + h * HEAD_DIM:HALF + (h + 1) * HEAD_DIM] for h in order], axis=0)
    mix = _dot(yc.astype(BF16), wout_ref[0:HALF]) + _dot(yd.astype(BF16), w_att)
    out = x_ref[0] + gt_ref[...] * mix
    if final:
        ms = jnp.mean(out * out, axis=-1, keepdims=True)
        out = out * lax.rsqrt(ms + EPS) * fn_ref[...]
    o_ref[0] = out


def _odd_out(x, mod, q, k, v, kc, vc, sink, z, cbg, sag, conv_w, w_out, final_norm, *, layer, mod_row, tq, window):
    b, l, _ = x.shape
    nt = l // tq
    nqb = tq // BLOCK
    lc = kc.shape[1]
    _, _, gt = _mod_specs(layer, mod_row)
    tile = lambda w: pl.BlockSpec((1, tq, w), lambda bi, t: (bi, t, 0))
    prev_blk = pl.BlockSpec((1, BLOCK, KV_WIDTH), lambda bi, t: (bi, jnp.maximum(t * nqb - 1, 0), 0))
    next_blk = pl.BlockSpec((1, BLOCK, KV_WIDTH), lambda bi, t: (bi, jnp.minimum((t + 1) * nqb, l // BLOCK - 1), 0))
    zh = tq // C_HALO
    args, specs = [q], [tile(HALF)]
    if window:
        args += [k, k, k, v, v, v]
        specs += [prev_blk, tile(KV_WIDTH), next_blk, prev_blk, tile(KV_WIDTH), next_blk]
    args += [kc, vc, sink, z, z, z, cbg, sag, x, mod, conv_w, w_out]
    specs += [pl.BlockSpec((1, lc, KV_WIDTH), lambda bi, t: (bi, 0, 0)),
              pl.BlockSpec((1, lc, KV_WIDTH), lambda bi, t: (bi, 0, 0)),
              pl.BlockSpec(memory_space=pltpu.SMEM),
              pl.BlockSpec((1, C_HALO, HALF), lambda bi, t: (bi, jnp.maximum(t * zh - 1, 0), 0)),
              tile(HALF),
              pl.BlockSpec((1, C_HALO, HALF), lambda bi, t: (bi, jnp.minimum((t + 1) * zh, l // C_HALO - 1), 0)),
              tile(HALF), tile(HALF), tile(D_MODEL), gt,
              _layer_spec(layer, (3, HALF)), _layer_spec(layer, (D_MODEL, D_MODEL))]
    if final_norm is not None:
        args.append(final_norm)
        specs.append(pl.BlockSpec((1, D_MODEL), lambda bi, t: (0, 0)))
    return pl.pallas_call(
        functools.partial(_odd_out_kernel, layer=layer, tq=tq, nt=nt, window=window, final=final_norm is not None),
        grid=(b, nt),
        in_specs=specs,
        out_specs=tile(D_MODEL),
        out_shape=jax.ShapeDtypeStruct((b, l, D_MODEL), F32),
        compiler_params=_params("arbitrary", "arbitrary"),
        name="odd_out_window" if window else "odd_out_context",
    )(*args)


def _rope_tables(l, grid_w):
    nf = HEAD_DIM // 4
    pos = np.arange(l)
    inv = np.power(ROPE_BASE, -np.arange(nf, dtype=np.float64) / nf)
    ang = np.stack([(pos // grid_w)[:, None] * inv, (pos % grid_w)[:, None] * inv], axis=1)
    cos, sin = np.cos(ang), np.sin(ang)
    cos_t = np.stack([cos, cos], axis=2).reshape(l, HEAD_DIM)
    sin_t = np.stack([-sin, sin], axis=2).reshape(l, HEAD_DIM)
    return jnp.asarray(np.tile(cos_t, (1, 2)), F32), jnp.asarray(np.tile(sin_t, (1, 2)), F32)


X_TILE = 512
GRID_W = 64


def kernel(x, c, ctx, c_ctx, e_norm, e_w_mod, e_b_mod, e_w_in, e_a_conv_w, e_a_conv_b, e_a_ln_g, e_a_ln_b,
           e_a_pw_w, e_a_pw_b, e_b_fw, e_w_out, o_norm, o_w_mod, o_b_mod, o_w_in, o_c_conv_w, o_sink, o_w_out,
           final_norm):
    b, l, _ = x.shape
    lc = ctx.shape[1]
    assert x.shape == (8, 4096, D_MODEL) and ctx.shape == (8, 256, D_MODEL)
    depth = e_norm.shape[0] + o_norm.shape[0]
    assert depth % 2 == 0

    cv = jnp.zeros((MOD_ROWS, D_MODEL), F32).at[:b].set(c).at[CTX_ROW].set(c_ctx)
    mod_e = _modulation(cv, e_w_mod, e_b_mod)
    mod_o = _modulation(cv, o_w_mod, o_b_mod)
    gmat = _fourier_channel_mats(e_b_fw)

    cos_x, sin_x = _rope_tables(l, GRID_W)
    cos_c, sin_c = jnp.ones((lc, 128), F32), jnp.zeros((lc, 128), F32)

    row = lambda a: a.reshape(a.shape[0], 1, a.shape[-1])
    rep = lambda a: jnp.broadcast_to(a[..., None, :], a.shape[:-1] + (SUBLANES, HALF))
    front = (row(e_norm), e_w_in.astype(BF16), rep(e_a_conv_w), rep(e_a_conv_b), row(e_a_ln_g), row(e_a_ln_b),
             e_a_pw_w.astype(BF16), row(e_a_pw_b), gmat)
    we_out, wo_in, wo_out = e_w_out.astype(BF16), o_w_in.astype(BF16), o_w_out.astype(BF16)
    sink = o_sink
    norm_o = row(o_norm)
    fin = final_norm.reshape(1, D_MODEL)

    xc = ctx
    for i in range(depth // 2):
        last = i == depth // 2 - 1
        xc, cbg_c, z_c, q_c, k_c, v_c, sag_c = _short_pair_front(
            xc, mod_e, *front, we_out, mod_o, norm_o, wo_in, cos_c, sin_c, layer=i, mod_row=CTX_ROW)
        ya, y, sgb = _even_front(x, mod_e, *front, layer=i, mod_row=None, tl=X_TILE)
        x, cbg, z, q, k, v, sag = _even_out_odd_in(
            y, sgb, ya, x, mod_e, we_out, mod_o, norm_o, wo_in, cos_x, sin_x, layer=i, mod_row=None, tl=X_TILE)
        x = _odd_out(x, mod_o, q, k, v, k_c, v_c, sink, z, cbg, sag, o_c_conv_w, wo_out, fin if last else None,
                     layer=i, mod_row=None, tq=X_TILE, window=True)
        if not last:
            xc = _odd_out(xc, mod_o, q_c, None, None, k_c, v_c, sink, z_c, cbg_c, sag_c, o_c_conv_w, wo_out, None,
                          layer=i, mod_row=CTX_ROW, tq=lc, window=False)
    return x
```

```python
import functools

import numpy as np
import jax
import jax.numpy as jnp
from jax import lax
from jax.experimental import pallas as pl
from jax.experimental.pallas import tpu as pltpu

D_MODEL = 1024
HALF = 512
A_CONV_W = 31
A_HALO = 16
C_HALO = 8
SUBLANES = 8
B_GROUPS = 4
B_GROUP_DIM = 128
HEAD_DIM = 64
N_Q_HEADS = 8
KV_WIDTH = 128
BLOCK = 128
ODD_IN = 3328
EVEN_IN = 2560
ROPE_BASE = 10000.0
NEG_INF = -1e30
LOG2E = 1.4426950408889634
Q_SCALE = HEAD_DIM ** -0.5 * LOG2E
EPS = 1e-6
RADIX = 16
MOD_ROWS = 16
CTX_ROW = 8
LANES = 128
MXU_COLS = 256
MXU_ROWS = 16
FFT_COLS = MXU_COLS
PM_SPLIT = ((), (0,), (1, 2), (3,))
PM_TAIL = (4, 5)
PM_ORDER = tuple(i for grp in PM_SPLIT for i in grp) + PM_TAIL
F32 = jnp.float32
BF16 = jnp.bfloat16
VMEM_LIMIT_BYTES = 48 * 1024 * 1024


def _params(*sem):
    return pltpu.CompilerParams(dimension_semantics=sem, vmem_limit_bytes=VMEM_LIMIT_BYTES)


def _dot(a, b):
    return jnp.dot(a, b, preferred_element_type=F32)


def _dot_nt(a, b):
    return lax.dot_general(a, b, (((1,), (1,)), ((), ())), preferred_element_type=F32)


def _pre_norm(x, g_ref, sh_ref, sc_ref):
    mult = g_ref[...] * (1.0 + sc_ref[...])
    ms = jnp.mean(x * x, axis=-1, keepdims=True)
    return (x * lax.rsqrt(ms + EPS)) * mult + sh_ref[...]


def _layer_spec(layer, shape):
    return pl.BlockSpec((None,) + tuple(shape), lambda bi, t: (layer,) + (0,) * len(shape))


def _mod_kernel(c_ref, w_ref, b_ref, o_ref):
    o_ref[0, :, 0, :] = _dot(jax.nn.silu(c_ref[...]), w_ref[0]) + b_ref[0]


def _modulation(cv, w_mod, b_mod):
    n = w_mod.shape[0]
    return pl.pallas_call(
        _mod_kernel,
        grid=(n, 3),
        in_specs=[pl.BlockSpec((MOD_ROWS, D_MODEL), lambda l, j: (0, 0)),
                  pl.BlockSpec((1, D_MODEL, D_MODEL), lambda l, j: (l, 0, j)),
                  pl.BlockSpec((1, 1, D_MODEL), lambda l, j: (l, 0, j))],
        out_specs=pl.BlockSpec((1, MOD_ROWS, 1, D_MODEL), lambda l, j: (l, 0, 0, j)),
        out_shape=jax.ShapeDtypeStruct((n, MOD_ROWS, 1, 3 * D_MODEL), F32),
        compiler_params=_params("arbitrary", "arbitrary"),
        name="modulation",
    )(cv, w_mod, b_mod.reshape(n, 1, 3 * D_MODEL))


def _mod_specs(layer, row):
    def spec(j):
        if row is None:
            return pl.BlockSpec((None, None, 1, D_MODEL), lambda b, t: (layer, b, 0, j))
        return pl.BlockSpec((None, None, 1, D_MODEL), lambda b, t: (layer, row, 0, j))
    return spec(0), spec(1), spec(2)


def _gprep_kernel(cd_ref, sd_ref, fw_ref, o_ref):
    o_ref[...] = jnp.zeros(o_ref.shape, o_ref.dtype)
    for g in range(B_GROUPS):
        fw = fw_ref[0, g]
        gc = jnp.dot(cd_ref[...], fw, preferred_element_type=F32, precision=lax.Precision.HIGHEST)
        gs = jnp.dot(sd_ref[...], fw, preferred_element_type=F32, precision=lax.Precision.HIGHEST)
        lo, hi = g * B_GROUP_DIM, (g + 1) * B_GROUP_DIM
        o_ref[0, lo:hi, lo:hi] = gc.astype(BF16)
        o_ref[0, lo:hi, HALF + lo:HALF + hi] = (-gs).astype(BF16)


def _fourier_channel_mats(fw):
    n = fw.shape[0]
    d = np.arange(B_GROUP_DIM)
    ang = 2.0 * np.pi * ((d[:, None] * d[None, :]) % B_GROUP_DIM) / B_GROUP_DIM
    cd = jnp.asarray(np.cos(ang) / np.sqrt(B_GROUP_DIM), F32)
    sd = jnp.asarray(np.sin(ang) / np.sqrt(B_GROUP_DIM), F32)
    mat = pl.BlockSpec((B_GROUP_DIM, B_GROUP_DIM), lambda i: (0, 0))
    return pl.pallas_call(
        _gprep_kernel,
        grid=(n,),
        in_specs=[mat, mat,
                  pl.BlockSpec((1, B_GROUPS, B_GROUP_DIM, B_GROUP_DIM), lambda i: (i, 0, 0, 0))],
        out_specs=pl.BlockSpec((1, HALF, 2 * HALF), lambda i: (i, 0, 0)),
        out_shape=jax.ShapeDtypeStruct((n, HALF, 2 * HALF), BF16),
        compiler_params=_params("arbitrary"),
        name="fourier_channel_mats",
    )(cd, sd, fw)


def _real_stack(m, real_only=False):
    top = np.concatenate([m.real, -m.imag], axis=1)
    if real_only:
        return top
    return np.concatenate([top, np.concatenate([m.imag, m.real], axis=1)], axis=0)


def _fft_tables_4096():
    r = np.arange(RADIX)
    eye = np.eye(RADIX)
    w16 = np.exp(-2j * np.pi * np.outer(r, r) / 16)
    w256 = np.exp(-2j * np.pi * np.outer(r, r) / 256)
    w4096 = np.exp(-2j * np.pi * np.outer(r, r) / 4096)
    ma = np.einsum("ka,kc,cd->kcad", w16, w4096, eye).reshape(256, 256) / 64.0
    mb = np.einsum("kb,kc,cd->kcbd", w16, w256, eye).reshape(256, 256)
    mc = np.einsum("kc,ae->kaec", w16, eye).reshape(256, 256)
    ph = 2.0 * np.pi * np.outer(r, r) / 256
    as_bf16 = lambda m: jnp.asarray(m, F32).astype(BF16)
    return (as_bf16(_real_stack(ma)), as_bf16(_real_stack(mb)), as_bf16(_real_stack(mc, real_only=True)),
            jnp.asarray(np.cos(ph), F32), jnp.asarray(np.sin(ph), F32))


def _fft_4096_kernel(z_ref, ma_ref, mb_ref, mc_ref, tc_ref, ts_ref, o_ref, s_scr):
    ct = z_ref.shape[-1]
    blk = RADIX * RADIX
    for b in range(RADIX):
        a1 = _dot(ma_ref[...], z_ref[0, :, :, b].reshape(2 * blk, ct))
        ar, ai = a1[:blk], a1[blk:]
        out_r, out_i = [], []
        for ka in range(RADIX):
            rr, ii = ar[ka * RADIX:(ka + 1) * RADIX], ai[ka * RADIX:(ka + 1) * RADIX]
            if b * ka == 0:
                out_r.append(rr)
                out_i.append(ii)
            else:
                c, s = tc_ref[b, ka], ts_ref[b, ka]
                out_r.append(rr * c + ii * s)
                out_i.append(ii * c - rr * s)
        s_scr[0, :, b] = jnp.concatenate(out_r, axis=0).astype(BF16).reshape(RADIX, RADIX, ct)
        s_scr[1, :, b] = jnp.concatenate(out_i, axis=0).astype(BF16).reshape(RADIX, RADIX, ct)
    for ka in range(RADIX):
        a2 = _dot(mb_ref[...], s_scr[:, ka].reshape(2 * blk, ct))
        s_scr[:, ka] = a2.astype(BF16).reshape(2, RADIX, RADIX, ct)
    for kb in range(RADIX):
        y = _dot(mc_ref[...], s_scr[:, :, kb].reshape(2 * blk, ct))
        o_ref[0, :, kb] = y.reshape(RADIX, RADIX, ct)


def _fft_4096(z):
    b = z.shape[0]
    ma, mb, mc, tc, ts = _fft_tables_4096()
    blk = RADIX * RADIX
    mat = lambda rows: pl.BlockSpec((rows, 2 * blk), lambda bi, j: (0, 0))
    smem = pl.BlockSpec(memory_space=pltpu.SMEM)
    y = pl.pallas_call(
        _fft_4096_kernel,
        grid=(b, HALF // FFT_COLS),
        in_specs=[pl.BlockSpec((1, 2, RADIX, RADIX, RADIX, FFT_COLS), lambda bi, j: (bi, 0, 0, 0, 0, j)),
                  mat(2 * blk), mat(2 * blk), mat(blk), smem, smem],
        out_specs=pl.BlockSpec((1, RADIX, RADIX, RADIX, FFT_COLS), lambda bi, j: (bi, 0, 0, 0, j)),
        out_shape=jax.ShapeDtypeStruct((b, RADIX, RADIX, RADIX, HALF), F32),
        scratch_shapes=[pltpu.VMEM((2, RADIX, RADIX, RADIX, FFT_COLS), BF16)],
        compiler_params=_params("arbitrary", "arbitrary"),
        name="fft_4096",
    )(z.reshape(b, 2, RADIX, RADIX, RADIX, HALF), ma, mb, mc, tc, ts)
    return y.reshape(b, RADIX ** 3, HALF)


def _zeros_once_computed(tiles, zero, rows):
    acc = None
    for tile in tiles:
        bits = lax.bitcast_convert_type(tile, jnp.int32)
        bits = bits.reshape(tile.shape[0] // SUBLANES, SUBLANES, tile.shape[1])
        for i in range(0, bits.shape[0], MXU_ROWS // SUBLANES):
            for c in range(0, tile.shape[1], MXU_COLS):
                part = bits[i, :, c:c + LANES]
                acc = part if acc is None else acc | part
    z = lax.bitcast_convert_type(acc & zero, F32)
    return jnp.broadcast_to(z[None], (rows // SUBLANES, SUBLANES, LANES)).reshape(rows, LANES)


def _shift_scratch(tl):
    return pltpu.VMEM((HALF // LANES, SUBLANES, tl + 2 * A_HALO + SUBLANES, LANES), F32)


def _even_in_body(xe, t, zero_ref, sh_ref, sc_ref, g_ref, win_ref, cw_ref, cb_ref, lg_ref, lb_ref, pw_ref, pb_ref,
                  gm_ref, u_scr, c_scr, *, tl, nt):
    ext = tl + 2 * A_HALO
    he = _pre_norm(xe, g_ref, sh_ref, sc_ref).astype(BF16)
    row = lax.broadcasted_iota(jnp.int32, (ext, 1), 0)
    inside = ((row >= A_HALO) | (t > 0)) & ((row < tl + A_HALO) | (t < nt - 1))
    sub = lax.broadcasted_iota(jnp.int32, (ext // SUBLANES, SUBLANES, LANES), 1)
    rc = 128
    off = A_HALO - (A_CONV_W - 1) // 2
    hm = he[A_HALO:A_HALO + tl]
    pm_tiles = []
    pad = jnp.zeros((ext, LANES), F32)
    for blk in range(HALF // LANES):
        lanes = slice(blk * LANES, (blk + 1) * LANES)
        w_blk = jnp.concatenate([win_ref[:, lanes], win_ref[:, HALF + blk * LANES:HALF + (blk + 1) * LANES]], axis=1)
        pa = _dot(he, w_blk)
        u = jnp.where(inside, pa[:, :LANES] * jax.nn.sigmoid(pa[:, LANES:]), pad)
        if PM_SPLIT[blk]:
            new = [_dot(hm, win_ref[:, 2 * HALF + i * MXU_COLS:2 * HALF + (i + 1) * MXU_COLS])
                   for i in PM_SPLIT[blk]]
            pm_tiles += new
            pad = _zeros_once_computed(new, zero_ref[0], ext)
        u_scr[blk, 0, 0:ext, :] = u
        u3 = u.reshape(ext // SUBLANES, SUBLANES, LANES)
        for r in range(1, SUBLANES):
            rot = pltpu.roll(u3, SUBLANES - r, 1)
            nxt = jnp.concatenate([rot[1:], rot[:1]], axis=0)
            u_scr[blk, r, 0:ext, :] = jnp.where(sub < SUBLANES - r, rot, nxt).reshape(ext, LANES)
        for r0 in range(0, tl, rc):
            acc = jnp.broadcast_to(cb_ref[:, lanes][None], (rc // SUBLANES, SUBLANES, LANES))
            for j in range(A_CONV_W):
                q, r = divmod(off + j, SUBLANES)
                uj = u_scr[blk, r, r0 + SUBLANES * q:r0 + SUBLANES * q + rc, :]
                acc = acc + uj.reshape(rc // SUBLANES, SUBLANES, LANES) * cw_ref[j, :, lanes][None]
            c_scr[r0:r0 + rc, lanes] = acc.reshape(rc, LANES)
    pm_tiles += [_dot(hm, win_ref[:, 2 * HALF + i * MXU_COLS:2 * HALF + (i + 1) * MXU_COLS]) for i in PM_TAIL]
    pm = jnp.concatenate([pm_tiles[PM_ORDER.index(i)] for i in range(len(PM_ORDER))], axis=1)
    pq = _dot(pm[:, HALF:2 * HALF].astype(BF16), gm_ref[...])
    tie = _zeros_once_computed(pm_tiles[-len(PM_TAIL):] + [pq], zero_ref[0], tl)
    cv = c_scr[...] + jnp.concatenate([tie] * (HALF // LANES), axis=1)
    mu = jnp.mean(cv, axis=-1, keepdims=True)
    cen = cv - mu
    var = jnp.mean(cen * cen, axis=-1, keepdims=True)
    ln = cen * lax.rsqrt(var + EPS) * lg_ref[...] + lb_ref[...]
    ya = (_dot(jax.nn.silu(ln).astype(BF16), pw_ref[...]) + pb_ref[...]) * jax.nn.silu(pm[:, :HALF])
    return ya.astype(BF16), pq, jax.nn.silu(pm[:, 2 * HALF:])


def _even_in_kernel(zero_ref, xp_ref, xm_ref, xn_ref, sh_ref, sc_ref, g_ref, win_ref, cw_ref, cb_ref, lg_ref, lb_ref,
                    pw_ref, pb_ref, gm_ref, ya_ref, z_ref, sgb_ref, u_scr, c_scr, *, tl, nt):
    xe = jnp.concatenate([xp_ref[0], xm_ref[0], xn_ref[0]], axis=0)
    ya, pq, sgb = _even_in_body(xe, pl.program_id(1), zero_ref, sh_ref, sc_ref, g_ref, win_ref, cw_ref, cb_ref,
                                lg_ref, lb_ref, pw_ref, pb_ref, gm_ref, u_scr, c_scr, tl=tl, nt=nt)
    ya_ref[0] = ya
    z_ref[0, 0] = pq[:, :HALF].astype(BF16)
    z_ref[0, 1] = pq[:, HALF:].astype(BF16)
    sgb_ref[0] = sgb


def _even_front(x, mod, norm_g, w_in, conv_w, conv_b, ln_g, ln_b, pw_w, pw_b, gmat, *, layer, mod_row, tl):
    b, l, _ = x.shape
    nt = l // tl
    hb = tl // A_HALO
    sh, sc, _ = _mod_specs(layer, mod_row)
    const = functools.partial(_layer_spec, layer)
    tile = lambda w: pl.BlockSpec((1, tl, w), lambda bi, t: (bi, t, 0))
    ya, z, sgb = pl.pallas_call(
        functools.partial(_even_in_kernel, tl=tl, nt=nt),
        grid=(b, nt),
        in_specs=[pl.BlockSpec(memory_space=pltpu.SMEM),
                  pl.BlockSpec((1, A_HALO, D_MODEL), lambda bi, t: (bi, jnp.maximum(t * hb - 1, 0), 0)),
                  tile(D_MODEL),
                  pl.BlockSpec((1, A_HALO, D_MODEL), lambda bi, t: (bi, jnp.minimum((t + 1) * hb, l // A_HALO - 1), 0)),
                  sh, sc, const((1, D_MODEL)), const((D_MODEL, EVEN_IN)), const((A_CONV_W, SUBLANES, HALF)),
                  const((SUBLANES, HALF)), const((1, HALF)), const((1, HALF)), const((HALF, HALF)), const((1, HALF)),
                  const((HALF, 2 * HALF))],
        out_specs=[tile(HALF),
                   pl.BlockSpec((1, 2, tl, HALF), lambda bi, t: (bi, 0, t, 0)),
                   tile(HALF)],
        out_shape=[jax.ShapeDtypeStruct((b, l, HALF), BF16),
                   jax.ShapeDtypeStruct((b, 2, l, HALF), BF16),
                   jax.ShapeDtypeStruct((b, l, HALF), F32)],
        scratch_shapes=[_shift_scratch(tl), pltpu.VMEM((tl, HALF), F32)],
        compiler_params=_params("arbitrary", "arbitrary"),
        name="even_in",
    )(jnp.zeros((1,), jnp.int32), x, x, x, mod, mod, norm_g, w_in, conv_w, conv_b, ln_g, ln_b, pw_w, pw_b, gmat)

    assert l == RADIX ** 3
    y = _fft_4096(z)
    return ya, y, sgb


def _rope(t, cos_ref, sin_ref):
    reps = t.shape[1] // 128
    c = jnp.concatenate([cos_ref[...]] * reps, axis=1) if reps > 1 else cos_ref[...]
    s = jnp.concatenate([sin_ref[...]] * reps, axis=1) if reps > 1 else sin_ref[...]
    lane = lax.broadcasted_iota(jnp.int32, t.shape, 1)
    partner = jnp.where((lane & 16) == 0, pltpu.roll(t, t.shape[1] - 16, 1), pltpu.roll(t, 16, 1))
    return t * c + partner * s


def _pair_heads(t):
    n = N_Q_HEADS // 2
    vregs = [t[:, j * LANES:(j + 1) * LANES] for j in range(n)]
    swapped = [pltpu.roll(v, HEAD_DIM, 1) for v in vregs]
    low = lax.broadcasted_iota(jnp.int32, vregs[0].shape, 1) < HEAD_DIM
    out = []
    for g in range(n):
        a, b = g // 2, (g + n) // 2
        out.append(jnp.where(low, vregs[a], swapped[b]) if g % 2 == 0 else jnp.where(low, swapped[a], vregs[b]))
    return jnp.concatenate(out, axis=1)


def _even_out_odd_in_body(y, sgb, ya, x, gt_ref, wout_ref, sh_ref, sc_ref, g_ref, win_ref, cos_ref, sin_ref,
                          xo_ref, cbg_ref, z_ref, q_ref, k_ref, v_ref, sag_ref):
    yb = (y * sgb).astype(BF16)
    cat = jnp.concatenate([ya, yb], axis=1)
    hr = x.shape[0] // 2
    xs, hs = [], []
    for rows in (slice(0, hr), slice(hr, 2 * hr)):
        xr = x[rows] + gt_ref[...] * _dot(cat[rows], wout_ref[...])
        xs.append(xr)
        hs.append(_pre_norm(xr, g_ref, sh_ref, sc_ref).astype(BF16))
    xo_ref[0] = jnp.concatenate(xs, axis=0)
    h = jnp.concatenate(hs, axis=0)
    q0 = 4 * HALF
    pa = _dot(h, win_ref[:, q0:ODD_IN])
    q = _rope(pa[:, 0:HALF], cos_ref, sin_ref) * Q_SCALE
    q_ref[0] = _pair_heads(q).astype(BF16)
    k_ref[0] = _rope(pa[:, HALF:HALF + KV_WIDTH], cos_ref, sin_ref).astype(BF16)
    v_ref[0] = pa[:, HALF + KV_WIDTH:HALF + 2 * KV_WIDTH].astype(BF16)
    sag_ref[0] = _pair_heads(jax.nn.silu(pa[:, HALF + 2 * KV_WIDTH:]))
    pc = _dot(h, win_ref[:, 0:q0])
    cbg_ref[0] = pc[:, 0:HALF] * jax.nn.silu(pc[:, 3 * HALF:4 * HALF])
    z_ref[0] = pc[:, HALF:2 * HALF] * pc[:, 2 * HALF:3 * HALF]


def _even_out_odd_in_kernel(y_ref, sgb_ref, ya_ref, x_ref, *refs):
    _even_out_odd_in_body(y_ref[0], sgb_ref[0], ya_ref[0], x_ref[0], *refs)


def _short_pair_front_kernel(zero_ref, x_ref, she_ref, sce_ref, gte_ref, ge_ref, wine_ref, cw_ref, cb_ref, lg_ref,
                             lb_ref, pw_ref, pb_ref, gm_ref, f_ref, woute_ref, sho_ref, sco_ref, go_ref, wino_ref,
                             cos_ref, sin_ref, xo_ref, cbg_ref, z_ref, q_ref, k_ref, v_ref, sag_ref, u_scr, c_scr, *,
                             tl):
    x = x_ref[0]
    xe = jnp.concatenate([x[:A_HALO], x, x[tl - A_HALO:]], axis=0)
    ya, pq, sgb = _even_in_body(xe, 0, zero_ref, she_ref, sce_ref, ge_ref, wine_ref, cw_ref, cb_ref, lg_ref, lb_ref,
                                pw_ref, pb_ref, gm_ref, u_scr, c_scr, tl=tl, nt=1)
    zri = jnp.concatenate([pq[:, :HALF].astype(BF16), pq[:, HALF:].astype(BF16)], axis=0)
    y = _dot(f_ref[...], zri)
    _even_out_odd_in_body(y, sgb, ya, x, gte_ref, woute_ref, sho_ref, sco_ref, go_ref, wino_ref, cos_ref, sin_ref,
                          xo_ref, cbg_ref, z_ref, q_ref, k_ref, v_ref, sag_ref)


def _short_pair_front(x, mod_e, norm_g, w_in, conv_w, conv_b, ln_g, ln_b, pw_w, pw_b, gmat, w_out, mod_o, norm_o,
                      wo_in, cos_t, sin_t, *, layer, mod_row):
    b, l, _ = x.shape
    she, sce, gte = _mod_specs(layer, mod_row)
    sho, sco, _ = _mod_specs(layer, mod_row)
    const = functools.partial(_layer_spec, layer)
    tile = lambda w: pl.BlockSpec((1, l, w), lambda bi, t: (bi, 0, 0))
    fixed = lambda shape: pl.BlockSpec(shape, lambda bi, t: (0,) * len(shape))
    k = np.arange(l)
    ang = 2.0 * np.pi * ((k[:, None] * k[None, :]) % l) / l
    f = jnp.asarray(np.concatenate([np.cos(ang), np.sin(ang)], axis=1) / np.sqrt(l), F32).astype(BF16)
    return pl.pallas_call(
        functools.partial(_short_pair_front_kernel, tl=l),
        grid=(b, 1),
        in_specs=[pl.BlockSpec(memory_space=pltpu.SMEM), tile(D_MODEL), she, sce, gte,
                  const((1, D_MODEL)), const((D_MODEL, EVEN_IN)), const((A_CONV_W, SUBLANES, HALF)),
                  const((SUBLANES, HALF)), const((1, HALF)), const((1, HALF)), const((HALF, HALF)), const((1, HALF)),
                  const((HALF, 2 * HALF)), fixed((l, 2 * l)), const((D_MODEL, D_MODEL)),
                  sho, sco, const((1, D_MODEL)), const((D_MODEL, ODD_IN)), fixed((l, 128)), fixed((l, 128))],
        out_specs=[tile(D_MODEL), tile(HALF), tile(HALF), tile(HALF), tile(KV_WIDTH), tile(KV_WIDTH), tile(HALF)],
        out_shape=[jax.ShapeDtypeStruct((b, l, D_MODEL), F32),
                   jax.ShapeDtypeStruct((b, l, HALF), F32), jax.ShapeDtypeStruct((b, l, HALF), F32),
                   jax.ShapeDtypeStruct((b, l, HALF), BF16), jax.ShapeDtypeStruct((b, l, KV_WIDTH), BF16),
                   jax.ShapeDtypeStruct((b, l, KV_WIDTH), BF16), jax.ShapeDtypeStruct((b, l, HALF), F32)],
        scratch_shapes=[_shift_scratch(l), pltpu.VMEM((l, HALF), F32)],
        compiler_params=_params("arbitrary", "arbitrary"),
        name="short_pair_front",
    )(jnp.zeros((1,), jnp.int32), x, mod_e, mod_e, mod_e, norm_g, w_in, conv_w, conv_b, ln_g, ln_b, pw_w, pw_b, gmat,
      f, w_out, mod_o, mod_o, norm_o, wo_in, cos_t, sin_t)


def _even_out_odd_in(y, sgb, ya, x, mod_e, w_out, mod_o, norm_g, w_in, cos_t, sin_t, *, layer, mod_row, tl):
    b, l, _ = x.shape
    _, _, gt = _mod_specs(layer, mod_row)
    sh, sc, _ = _mod_specs(layer, mod_row)
    tile = lambda w: pl.BlockSpec((1, tl, w), lambda bi, t: (bi, t, 0))
    const = functools.partial(_layer_spec, layer)
    return pl.pallas_call(
        _even_out_odd_in_kernel,
        grid=(b, l // tl),
        in_specs=[tile(HALF), tile(HALF), tile(HALF), tile(D_MODEL), gt, const((D_MODEL, D_MODEL)),
                  sh, sc, const((1, D_MODEL)), const((D_MODEL, ODD_IN)),
                  pl.BlockSpec((tl, 128), lambda bi, t: (t, 0)),
                  pl.BlockSpec((tl, 128), lambda bi, t: (t, 0))],
        out_specs=[tile(D_MODEL), tile(HALF), tile(HALF), tile(HALF), tile(KV_WIDTH), tile(KV_WIDTH), tile(HALF)],
        out_shape=[jax.ShapeDtypeStruct((b, l, D_MODEL), F32),
                   jax.ShapeDtypeStruct((b, l, HALF), F32), jax.ShapeDtypeStruct((b, l, HALF), F32),
                   jax.ShapeDtypeStruct((b, l, HALF), BF16), jax.ShapeDtypeStruct((b, l, KV_WIDTH), BF16),
                   jax.ShapeDtypeStruct((b, l, KV_WIDTH), BF16), jax.ShapeDtypeStruct((b, l, HALF), F32)],
        compiler_params=_params("arbitrary", "arbitrary"),
        name="even_out_odd_in",
    )(y, sgb, ya, x, mod_e, w_out, mod_o, mod_o, norm_g, w_in, cos_t, sin_t)


def _odd_out_kernel(*refs, layer, tq, nt, window, final):
    refs = list(refs)
    q_ref = refs.pop(0)
    if window:
        kp_ref, km_ref, kn_ref, vp_ref, vm_ref, vn_ref = refs[:6]
        refs = refs[6:]
    kc_ref, vc_ref, sink_ref, zp_ref, zm_ref, zn_ref, cbg_ref, sag_ref, x_ref, gt_ref, cw_ref, wout_ref = refs[:12]
    refs = refs[12:]
    fn_ref = refs.pop(0) if final else None
    o_ref = refs.pop(0)

    t = pl.program_id(1)
    nqb = tq // BLOCK
    lane = lax.broadcasted_iota(jnp.int32, (BLOCK, 128), 1)
    low = lane < HEAD_DIM
    kc, vc = kc_ref[0], vc_ref[0]
    if window:
        kext = jnp.concatenate([kp_ref[0], km_ref[0], kn_ref[0]], axis=0)
        vext = jnp.concatenate([vp_ref[0], vm_ref[0], vn_ref[0]], axis=0)
        qi = lax.broadcasted_iota(jnp.int32, (BLOCK, 3 * BLOCK), 0)
        kj = lax.broadcasted_iota(jnp.int32, (BLOCK, 3 * BLOCK), 1)
        band = (kj - qi >= 0) & (kj - qi <= 2 * BLOCK)

    att_rows = []
    zero = jnp.zeros((BLOCK, LANES), BF16)
    heads_per_kv = N_Q_HEADS // 2
    for qb in range(nqb):
        qg = q_ref[0, qb * BLOCK:(qb + 1) * BLOCK, :]
        slots = []
        for g in range(heads_per_kv):
            qq = qg[:, g * LANES:(g + 1) * LANES]
            slots += [jnp.where(low, qq, zero), jnp.where(low, zero, qq)]
        qs = jnp.concatenate(slots, axis=0)
        s_c = _dot_nt(qs, kc)
        if window:
            s_w = _dot_nt(qs, kext[qb * BLOCK:(qb + 3) * BLOCK])
            valid = band
            if qb == 0:
                valid = valid & ((kj >= BLOCK) | (t > 0))
            if qb == nqb - 1:
                valid = valid & ((kj < 2 * BLOCK) | (t < nt - 1))
        pws, pcs, dens = [], [], []
        for s in range(N_Q_HEADS):
            head = s // 2 + heads_per_kv * (s % 2)
            sink = sink_ref[layer, head] * LOG2E
            sc_s = s_c[s * BLOCK:(s + 1) * BLOCK]
            m = jnp.maximum(jnp.max(sc_s, axis=-1, keepdims=True), sink)
            if window:
                sw_s = jnp.where(valid, s_w[s * BLOCK:(s + 1) * BLOCK], NEG_INF)
                m = jnp.maximum(m, jnp.max(sw_s, axis=-1, keepdims=True))
            pc = jnp.exp2(sc_s - m)
            den = jnp.exp2(sink - m) + jnp.sum(pc, axis=-1, keepdims=True)
            if window:
                pw = jnp.exp2(sw_s - m)
                den = den + jnp.sum(pw, axis=-1, keepdims=True)
                pws.append(pw.astype(BF16))
            pcs.append(pc.astype(BF16))
            dens.append(den)
        o = _dot(jnp.concatenate(pcs, axis=0), vc)
        if window:
            o = o + _dot(jnp.concatenate(pws, axis=0), vext[qb * BLOCK:(qb + 3) * BLOCK])
        o = o / jnp.concatenate(dens, axis=0)
        att_rows.append(jnp.concatenate(
            [jnp.where(low, o[(2 * g) * BLOCK:(2 * g + 1) * BLOCK], o[(2 * g + 1) * BLOCK:(2 * g + 2) * BLOCK])
             for g in range(heads_per_kv)], axis=1))
    att = jnp.concatenate(att_rows, axis=0) if nqb > 1 else att_rows[0]
    yd = att * sag_ref[0]

    zp = jnp.where(t > 0, zp_ref[0], 0.0)
    zn = jnp.where(t < nt - 1, zn_ref[0], 0.0)
    zext = jnp.concatenate([zp, zm_ref[0], zn], axis=0)
    conv = (zext[C_HALO - 1:C_HALO - 1 + tq] * cw_ref[0:1, :] + zext[C_HALO:C_HALO + tq] * cw_ref[1:2, :]
            + zext[C_HALO + 1:C_HALO + 1 + tq] * cw_ref[2:3, :])
    yc = cbg_ref[0] * conv
    order = [g + (N_Q_HEADS // 2) * half for g in range(N_Q_HEADS // 2) for half in range(2)]
    w_att = jnp.concatenate([wout_ref[HALF + h * HEAD_DIM:HALF + (h + 1) * HEAD_DIM] for h in order], axis=0)
    mix = _dot(yc.astype(BF16), wout_ref[0:HALF]) + _dot(yd.astype(BF16), w_att)
    out = x_ref[0] + gt_ref[...] * mix
    if final:
        ms = jnp.mean(out * out, axis=-1, keepdims=True)
        out = out * lax.rsqrt(ms + EPS) * fn_ref[...]
    o_ref[0] = out


def _odd_out(x, mod, q, k, v, kc, vc, sink, z, cbg, sag, conv_w, w_out, final_norm, *, layer, mod_row, tq, window):
    b, l, _ = x.shape
    nt = l // tq
    nqb = tq // BLOCK
    lc = kc.shape[1]
    _, _, gt = _mod_specs(layer, mod_row)
    tile = lambda w: pl.BlockSpec((1, tq, w), lambda bi, t: (bi, t, 0))
    prev_blk = pl.BlockSpec((1, BLOCK, KV_WIDTH), lambda bi, t: (bi, jnp.maximum(t * nqb - 1, 0), 0))
    next_blk = pl.BlockSpec((1, BLOCK, KV_WIDTH), lambda bi, t: (bi, jnp.minimum((t + 1) * nqb, l // BLOCK - 1), 0))
    zh = tq // C_HALO
    args, specs = [q], [tile(HALF)]
    if window:
        args += [k, k, k, v, v, v]
        specs += [prev_blk, tile(KV_WIDTH), next_blk, prev_blk, tile(KV_WIDTH), next_blk]
    args += [kc, vc, sink, z, z, z, cbg, sag, x, mod, conv_w, w_out]
    specs += [pl.BlockSpec((1, lc, KV_WIDTH), lambda bi, t: (bi, 0, 0)),
              pl.BlockSpec((1, lc, KV_WIDTH), lambda bi, t: (bi, 0, 0)),
              pl.BlockSpec(memory_space=pltpu.SMEM),
              pl.BlockSpec((1, C_HALO, HALF), lambda bi, t: (bi, jnp.maximum(t * zh - 1, 0), 0)),
              tile(HALF),
              pl.BlockSpec((1, C_HALO, HALF), lambda bi, t: (bi, jnp.minimum((t + 1) * zh, l // C_HALO - 1), 0)),
              tile(HALF), tile(HALF), tile(D_MODEL), gt,
              _layer_spec(layer, (3, HALF)), _layer_spec(layer, (D_MODEL, D_MODEL))]
    if final_norm is not None:
        args.append(final_norm)
        specs.append(pl.BlockSpec((1, D_MODEL), lambda bi, t: (0, 0)))
    return pl.pallas_call(
        functools.partial(_odd_out_kernel, layer=layer, tq=tq, nt=nt, window=window, final=final_norm is not None),
        grid=(b, nt),
        in_specs=specs,
        out_specs=tile(D_MODEL),
        out_shape=jax.ShapeDtypeStruct((b, l, D_MODEL), F32),
        compiler_params=_params("arbitrary", "arbitrary"),
        name="odd_out_window" if window else "odd_out_context",
    )(*args)


def _rope_tables(l, grid_w):
    nf = HEAD_DIM // 4
    pos = np.arange(l)
    inv = np.power(ROPE_BASE, -np.arange(nf, dtype=np.float64) / nf)
    ang = np.stack([(pos // grid_w)[:, None] * inv, (pos % grid_w)[:, None] * inv], axis=1)
    cos, sin = np.cos(ang), np.sin(ang)
    cos_t = np.stack([cos, cos], axis=2).reshape(l, HEAD_DIM)
    sin_t = np.stack([-sin, sin], axis=2).reshape(l, HEAD_DIM)
    return jnp.asarray(np.tile(cos_t, (1, 2)), F32), jnp.asarray(np.tile(sin_t, (1, 2)), F32)


X_TILE = 512
GRID_W = 64


def kernel(x, c, ctx, c_ctx, e_norm, e_w_mod, e_b_mod, e_w_in, e_a_conv_w, e_a_conv_b, e_a_ln_g, e_a_ln_b,
           e_a_pw_w, e_a_pw_b, e_b_fw, e_w_out, o_norm, o_w_mod, o_b_mod, o_w_in, o_c_conv_w, o_sink, o_w_out,
           final_norm):
    b, l, _ = x.shape
    lc = ctx.shape[1]
    assert x.shape == (8, 4096, D_MODEL) and ctx.shape == (8, 256, D_MODEL)
    depth = e_norm.shape[0] + o_norm.shape[0]
    assert depth % 2 == 0

    cv = jnp.zeros((MOD_ROWS, D_MODEL), F32).at[:b].set(c).at[CTX_ROW].set(c_ctx)
    mod_e = _modulation(cv, e_w_mod, e_b_mod)
    mod_o = _modulation(cv, o_w_mod, o_b_mod)
    gmat = _fourier_channel_mats(e_b_fw)

    cos_x, sin_x = _rope_tables(l, GRID_W)
    cos_c, sin_c = jnp.ones((lc, 128), F32), jnp.zeros((lc, 128), F32)

    row = lambda a: a.reshape(a.shape[0], 1, a.shape[-1])
    rep = lambda a: jnp.broadcast_to(a[..., None, :], a.shape[:-1] + (SUBLANES, HALF))
    front = (row(e_norm), e_w_in.astype(BF16), rep(e_a_conv_w), rep(e_a_conv_b), row(e_a_ln_g), row(e_a_ln_b),
             e_a_pw_w.astype(BF16), row(e_a_pw_b), gmat)
    we_out, wo_in, wo_out = e_w_out.astype(BF16), o_w_in.astype(BF16), o_w_out.astype(BF16)
    sink = o_sink
    norm_o = row(o_norm)
    fin = final_norm.reshape(1, D_MODEL)

    xc = ctx
    for i in range(depth // 2):
        last = i == depth // 2 - 1
        xc, cbg_c, z_c, q_c, k_c, v_c, sag_c = _short_pair_front(
            xc, mod_e, *front, we_out, mod_o, norm_o, wo_in, cos_c, sin_c, layer=i, mod_row=CTX_ROW)
        ya, y, sgb = _even_front(x, mod_e, *front, layer=i, mod_row=None, tl=X_TILE)
        x, cbg, z, q, k, v, sag = _even_out_odd_in(
            y, sgb, ya, x, mod_e, we_out, mod_o, norm_o, wo_in, cos_x, sin_x, layer=i, mod_row=None, tl=X_TILE)
        x = _odd_out(x, mod_o, q, k, v, k_c, v_c, sink, z, cbg, sag, o_c_conv_w, wo_out, fin if last else None,
                     layer=i, mod_row=None, tq=X_TILE, window=True)
        if not last:
            xc = _odd_out(xc, mod_o, q_c, None, None, k_c, v_c, sink, z_c, cbg_c, sag_c, o_c_conv_w, wo_out, None,
                          layer=i, mod_row=CTX_ROW, tq=lc, window=False)
    return x
```
